```python
import jax, jax.numpy as jnp
from jax import lax
import numpy as np

D_MODEL = 1024
BATCH = 8
SEQ = 2048
DEPTH = 2

CHUNK = 64
N_MEM = 256
HEAD_DIM = 64
MIX_WIDTH = D_MODEL
N_MEM_HEADS = 4
MEM_WIDTH = N_MEM_HEADS * HEAD_DIM
TOK_WIDTH = MIX_WIDTH - MEM_WIDTH
N_FOX_HEADS = TOK_WIDTH // HEAD_DIM
GMLP_BLOCK = 128
GMLP_GROUPS = 4
GMLP_GROUP_WIDTH = TOK_WIDTH // GMLP_GROUPS
D_FF = 2816
CONV_WIDTH = 3
Q_BLOCK = 128
N_A = DEPTH // 2
N_B = DEPTH - N_A
EPS = 1e-6

kernel_name = "yoco_gmlp_fox_hybrid"


def rmsnorm(x, g):
    xf = x.astype(jnp.float32)
    y = xf * lax.rsqrt(jnp.mean(xf * xf, axis=-1, keepdims=True) + EPS)
    return (y * g.astype(jnp.float32)).astype(x.dtype)


def memory_attention(q_mem, mem, mem_norm, w_mem_kv):
    b, s = q_mem.shape[:2]
    kv = rmsnorm(mem, mem_norm) @ w_mem_kv
    k = kv[..., :MEM_WIDTH].reshape(b, N_MEM, N_MEM_HEADS, HEAD_DIM)
    v = kv[..., MEM_WIDTH:].reshape(b, N_MEM, N_MEM_HEADS, HEAD_DIM)
    q = q_mem.reshape(b, s, N_MEM_HEADS, HEAD_DIM)
    logits = jnp.einsum("bqhd,bkhd->bhqk", q, k).astype(jnp.float32) * (HEAD_DIM ** -0.5)
    p = jax.nn.softmax(logits, axis=-1).astype(v.dtype)
    o = jnp.einsum("bhqk,bkhd->bqhd", p, v)
    return o.reshape(b, s, MEM_WIDTH)


def gmlp_spatial_gating(u, v, v_norm, w_s, b_s):
    b, s, _ = v.shape
    n = s // GMLP_BLOCK
    vn = rmsnorm(v, v_norm).reshape(b, n, GMLP_BLOCK, GMLP_GROUPS, GMLP_GROUP_WIDTH)
    causal = jnp.tril(jnp.ones((GMLP_BLOCK, GMLP_BLOCK), dtype=bool))
    w = jnp.where(causal[None], w_s, jnp.zeros_like(w_s))
    mixed = jnp.einsum("gts,bnsgc->bntgc", w, vn) + b_s.T[None, None, :, :, None]
    return u * mixed.reshape(b, s, TOK_WIDTH)


def forgetting_attention(q, k, v, log_f_cum):
    s = q.shape[1]
    scale = HEAD_DIM ** -0.5
    c = jnp.transpose(log_f_cum, (0, 2, 1))
    outs = []
    for i in range(s // Q_BLOCK):
        q0, q1 = i * Q_BLOCK, (i + 1) * Q_BLOCK
        logits = jnp.einsum("bqhd,bkhd->bhqk", q[:, q0:q1], k[:, :q1]).astype(jnp.float32) * scale
        decay = c[:, :, q0:q1, None] - c[:, :, None, :q1]
        qpos = jnp.arange(q0, q1)[:, None]
        kpos = jnp.arange(q1)[None, :]
        logits = jnp.where(qpos >= kpos, logits + decay, -jnp.inf)
        p = jax.nn.softmax(logits, axis=-1).astype(v.dtype)
        outs.append(jnp.einsum("bhqk,bkhd->bqhd", p, v[:, :q1]))
    return jnp.concatenate(outs, axis=1)


def conv_ffn(x, w_in, conv_w, conv_b, w_out):
    s = x.shape[1]
    h = x @ w_in
    hp = jnp.pad(h, ((0, 0), (CONV_WIDTH - 1, 0), (0, 0)))
    hc = conv_b + conv_w[CONV_WIDTH - 1] * h
    for j in range(CONV_WIDTH - 1):
        hc = hc + conv_w[j] * hp[:, j:j + s]
    gate, up = hc[..., :D_FF], hc[..., D_FF:]
    return (jax.nn.silu(gate) * up) @ w_out


def setup_inputs(seed: int = 0) -> dict:
    key = jax.random.key(seed)
    ks = iter(jax.random.split(key, 64))

    def nrm(shape, scale):
        return jax.random.normal(next(ks), shape, jnp.float32) * scale

    def gain(shape):
        return 1.0 + nrm(shape, 0.05)

    D = D_MODEL
    inv = D ** -0.5
    inp = {}
    inp["x"] = nrm((BATCH, SEQ, D), 1.0)
    inp["mem"] = nrm((BATCH, N_MEM, D), 1.0)
    inp["a_norm1"] = gain((N_A, D))
    inp["a_w_in"] = nrm((N_A, D, 2 * TOK_WIDTH + MEM_WIDTH), inv)
    inp["a_v_norm"] = gain((N_A, TOK_WIDTH))
    inp["a_w_s"] = nrm((N_A, GMLP_GROUPS, GMLP_BLOCK, GMLP_BLOCK), 0.5 * GMLP_BLOCK ** -0.5)
    inp["a_b_s"] = 1.0 + nrm((N_A, GMLP_GROUPS, GMLP_BLOCK), 0.1)
    inp["a_mem_norm"] = gain((N_A, D))
    inp["a_w_mem_kv"] = nrm((N_A, D, 2 * MEM_WIDTH), inv)
    inp["a_w_out"] = nrm((N_A, MIX_WIDTH, D), MIX_WIDTH ** -0.5)
    inp["a_norm2"] = gain((N_A, D))
    inp["a_ffn_in"] = nrm((N_A, D, 2 * D_FF), inv)
    inp["a_ffn_conv"] = nrm((N_A, CONV_WIDTH, 2 * D_FF), CONV_WIDTH ** -0.5)
    inp["a_ffn_conv_b"] = nrm((N_A, 2 * D_FF), 0.02)
    inp["a_ffn_out"] = nrm((N_A, D_FF, D), D_FF ** -0.5)
    inp["kv_norm"] = gain((D,))
    inp["w_kv"] = nrm((D, 2 * TOK_WIDTH + N_FOX_HEADS), inv)
    inp["b_f"] = jax.random.uniform(next(ks), (N_FOX_HEADS,), jnp.float32, 1.0, 6.0)
    inp["b_norm1"] = gain((N_B, D))
    inp["b_w_q"] = nrm((N_B, D, TOK_WIDTH + MEM_WIDTH), inv)
    inp["b_mem_norm"] = gain((N_B, D))
    inp["b_w_mem_kv"] = nrm((N_B, D, 2 * MEM_WIDTH), inv)
    inp["b_w_out"] = nrm((N_B, MIX_WIDTH, D), MIX_WIDTH ** -0.5)
    inp["b_norm2"] = gain((N_B, D))
    inp["b_ffn_in"] = nrm((N_B, D, 2 * D_FF), inv)
    inp["b_ffn_conv"] = nrm((N_B, CONV_WIDTH, 2 * D_FF), CONV_WIDTH ** -0.5)
    inp["b_ffn_conv_b"] = nrm((N_B, 2 * D_FF), 0.02)
    inp["b_ffn_out"] = nrm((N_B, D_FF, D), D_FF ** -0.5)
    inp["final_norm"] = gain((D,))
    return inp


def reference(x, mem,
              a_norm1, a_w_in, a_v_norm, a_w_s, a_b_s, a_mem_norm, a_w_mem_kv, a_w_out,
              a_norm2, a_ffn_in, a_ffn_conv, a_ffn_conv_b, a_ffn_out,
              kv_norm, w_kv, b_f,
              b_norm1, b_w_q, b_mem_norm, b_w_mem_kv, b_w_out,
              b_norm2, b_ffn_in, b_ffn_conv, b_ffn_conv_b, b_ffn_out,
              final_norm):
    b, s, _ = x.shape
    k_sh = v_sh = log_f_cum = None
    for layer in range(DEPTH):
        if layer < N_A:
            i = layer
            h = rmsnorm(x, a_norm1[i])
            z = h @ a_w_in[i]
            u = jax.nn.gelu(z[..., :TOK_WIDTH])
            vv = jax.nn.gelu(z[..., TOK_WIDTH:2 * TOK_WIDTH])
            q_mem = z[..., 2 * TOK_WIDTH:]
            tok = gmlp_spatial_gating(u, vv, a_v_norm[i], a_w_s[i], a_b_s[i])
            mem_o = memory_attention(q_mem, mem, a_mem_norm[i], a_w_mem_kv[i])
            x = x + jnp.concatenate([tok, mem_o], axis=-1) @ a_w_out[i]
            x = x + conv_ffn(rmsnorm(x, a_norm2[i]), a_ffn_in[i], a_ffn_conv[i], a_ffn_conv_b[i], a_ffn_out[i])
            if layer == N_A - 1:
                kvf = rmsnorm(x, kv_norm) @ w_kv
                k_sh = kvf[..., :TOK_WIDTH].reshape(b, s, N_FOX_HEADS, HEAD_DIM)
                v_sh = kvf[..., TOK_WIDTH:2 * TOK_WIDTH].reshape(b, s, N_FOX_HEADS, HEAD_DIM)
                f_logit = kvf[..., 2 * TOK_WIDTH:].astype(jnp.float32) + b_f.astype(jnp.float32)
                log_f_cum = jnp.cumsum(jax.nn.log_sigmoid(f_logit), axis=1)
        else:
            j = layer - N_A
            h = rmsnorm(x, b_norm1[j])
            z = h @ b_w_q[j]
            q = z[..., :TOK_WIDTH].reshape(b, s, N_FOX_HEADS, HEAD_DIM)
            q_mem = z[..., TOK_WIDTH:]
            tok = forgetting_attention(q, k_sh, v_sh, log_f_cum).reshape(b, s, TOK_WIDTH)
            mem_o = memory_attention(q_mem, mem, b_mem_norm[j], b_w_mem_kv[j])
            x = x + jnp.concatenate([tok, mem_o], axis=-1) @ b_w_out[j]
            x = x + conv_ffn(rmsnorm(x, b_norm2[j]), b_ffn_in[j], b_ffn_conv[j], b_ffn_conv_b[j], b_ffn_out[j])
    return rmsnorm(x, final_norm)
```

```python
import functools

import jax
import jax.numpy as jnp
from jax import lax
from jax.experimental import pallas as pl
from jax.experimental.pallas import tpu as pltpu

D_MODEL = 1024
N_MEM = 256
HEAD_DIM = 64
N_MEM_HEADS = 4
MEM_WIDTH = N_MEM_HEADS * HEAD_DIM
TOK_WIDTH = D_MODEL - MEM_WIDTH
N_FOX_HEADS = TOK_WIDTH // HEAD_DIM
GMLP_BLOCK = 128
GMLP_GROUPS = 4
D_FF = 2816
CONV_WIDTH = 3
EPS = 1e-6
Q_SCALE = HEAD_DIM ** -0.5

LANES = 128
SUBLANES = 8
HEAD_PAIR = 2 * HEAD_DIM
FF_CHUNK = 256
VMEM_LIMIT = 56 * 1024 * 1024

BF16 = jnp.bfloat16
F32 = jnp.float32


def _const_spec(shape):
    nd = len(shape)
    return pl.BlockSpec(shape, lambda *_: (0,) * nd, pipeline_mode=pl.Buffered(1))


def _params(n_axes):
    return pltpu.CompilerParams(
        dimension_semantics=("arbitrary",) * n_axes, vmem_limit_bytes=VMEM_LIMIT)


def _rms(x, g):
    r = lax.rsqrt(jnp.mean(x * x, axis=-1, keepdims=True) + EPS)
    return x * r * g


def _dot(a, b):
    return jnp.dot(a, b, preferred_element_type=F32)


def _dot_nt(a, b):
    return lax.dot_general(a, b, (((1,), (1,)), ((), ())), preferred_element_type=F32)


def _memkv_kernel(mem_ref, g_ref, w_ref, kbd_ref, vbd_ref):
    mn = _rms(mem_ref[0], g_ref[0]).astype(BF16)
    kv = _dot(mn, w_ref[0])
    k, v = kv[:, :MEM_WIDTH], kv[:, MEM_WIDTH:]
    feat_head = lax.broadcasted_iota(jnp.int32, (N_MEM, MEM_WIDTH), 1) // HEAD_DIM
    for h in range(N_MEM_HEADS):
        rows = slice(h * N_MEM, (h + 1) * N_MEM)
        kbd_ref[0, 0, rows, :] = jnp.where(feat_head == h, k, 0.0).astype(BF16)
        vbd_ref[0, 0, rows, :] = jnp.where(feat_head == h, v, 0.0).astype(BF16)


def _memkv(mem, norms, ws):
    n_layers, batch = norms.shape[0], mem.shape[0]
    out = jax.ShapeDtypeStruct((n_layers, batch, N_MEM_HEADS * N_MEM, MEM_WIDTH), BF16)
    out_spec = pl.BlockSpec((1, 1, N_MEM_HEADS * N_MEM, MEM_WIDTH), lambda l, b: (l, b, 0, 0))
    return pl.pallas_call(
        _memkv_kernel,
        grid=(n_layers, batch),
        in_specs=[
            pl.BlockSpec((1, N_MEM, D_MODEL), lambda l, b: (b, 0, 0)),
            pl.BlockSpec((1, 1, D_MODEL), lambda l, b: (l, 0, 0)),
            pl.BlockSpec((1, D_MODEL, 2 * MEM_WIDTH), lambda l, b: (l, 0, 0)),
        ],
        out_specs=[out_spec, out_spec],
        out_shape=[out, out],
        compiler_params=_params(2),
        name="memkv",
    )(mem, norms, ws)


def _mem_attention(q, kbd_ref, vbd_ref):
    logits = _dot_nt(q, kbd_ref[0, 0])
    probs = []
    for h in range(N_MEM_HEADS):
        seg = logits[:, h * N_MEM:(h + 1) * N_MEM]
        e = jnp.exp(seg - jnp.max(seg, axis=-1, keepdims=True))
        probs.append((e / jnp.sum(e, axis=-1, keepdims=True)).astype(BF16))
    return _dot(jnp.concatenate(probs, axis=-1), vbd_ref[0, 0])


def _gmlp_mix(vn, ws_ref):
    tm = vn.shape[0]
    tri = (lax.broadcasted_iota(jnp.int32, (GMLP_BLOCK, GMLP_BLOCK), 0)
           >= lax.broadcasted_iota(jnp.int32, (GMLP_BLOCK, GMLP_BLOCK), 1))
    w = [jnp.where(tri, ws_ref[g], 0.0).astype(BF16) for g in range(GMLP_GROUPS)]
    first_half = lax.broadcasted_iota(jnp.int32, (GMLP_BLOCK, LANES), 1) < HEAD_DIM
    win_lo = (0, 128, 384, 512)
    blocks = []
    for r in range(tm // GMLP_BLOCK):
        rows = slice(r * GMLP_BLOCK, (r + 1) * GMLP_BLOCK)
        m = [_dot(w[g], vn[rows, win_lo[g]:win_lo[g] + 2 * LANES]) for g in range(GMLP_GROUPS)]
        blocks.append(jnp.concatenate([
            m[0][:, :LANES],
            jnp.where(first_half, m[0][:, LANES:], m[1][:, :LANES]),
            m[1][:, LANES:],
            m[2][:, :LANES],
            jnp.where(first_half, m[2][:, LANES:], m[3][:, :LANES]),
            m[3][:, LANES:],
        ], axis=-1))
    return jnp.concatenate(blocks, axis=0)


def _mixer_a_kernel(x_ref, g1_ref, win_ref, vg_ref, ws_ref, bs_ref, kbd_ref, vbd_ref, wout_ref,
                    o_ref):
    x = x_ref[0]
    tm = x.shape[0]
    h = _rms(x, g1_ref[...]).astype(BF16)
    z = _dot(h, win_ref[...])
    u = jax.nn.gelu(z[:, :TOK_WIDTH])
    v = jax.nn.gelu(z[:, TOK_WIDTH:2 * TOK_WIDTH])
    q_mem = (z[:, 2 * TOK_WIDTH:] * Q_SCALE).astype(BF16)
    vn = _rms(v, vg_ref[...]).astype(BF16)
    bias = jnp.concatenate([bs_ref[...]] * (tm // GMLP_BLOCK), axis=0)
    tok = (u * (_gmlp_mix(vn, ws_ref) + bias)).astype(BF16)
    mem_o = _mem_attention(q_mem, kbd_ref, vbd_ref).astype(BF16)
    cat = jnp.concatenate([tok, mem_o], axis=-1)
    o_ref[0] = x + _dot(cat, wout_ref[...])


def _mixer_a(x, g1, w_in, v_g, w_s, b_full, kvbd, layer, w_out, tm):
    batch, seq, _ = x.shape
    xspec = pl.BlockSpec((1, tm, D_MODEL), lambda b, s: (b, s, 0))
    kvspec = pl.BlockSpec((1, 1, N_MEM_HEADS * N_MEM, MEM_WIDTH), lambda b, s: (layer, b, 0, 0))
    return pl.pallas_call(
        _mixer_a_kernel,
        grid=(batch, seq // tm),
        in_specs=[
            xspec,
            _const_spec((1, D_MODEL)),
            _const_spec(w_in.shape),
            _const_spec((1, TOK_WIDTH)),
            _const_spec(w_s.shape),
            _const_spec(b_full.shape),
            kvspec, kvspec,
            _const_spec(w_out.shape),
        ],
        out_specs=xspec,
        out_shape=jax.ShapeDtypeStruct(x.shape, F32),
        compiler_params=_params(2),
        name="mixer_a",
    )(x, g1, w_in, v_g, w_s, b_full, kvbd[0], kvbd[1], w_out)


def _shift_rows(h, prev, n):
    rolled = pltpu.roll(h, n, 0)
    head = jnp.where(lax.broadcasted_iota(jnp.int32, prev.shape, 0) < n,
                     pltpu.roll(prev, n, 0), rolled[:SUBLANES])
    return jnp.concatenate([head, rolled[SUBLANES:]], axis=0)


def _ffn_kernel(x_ref, g_ref, win_ref, cw_ref, cb_ref, wout_ref, fn_ref, o_ref, carry_ref,
                *, final_norm):
    @pl.when(pl.program_id(1) == 0)
    def _():
        carry_ref[...] = jnp.zeros_like(carry_ref)

    x = x_ref[0]
    tm = x.shape[0]
    hn = _rms(x, g_ref[...]).astype(BF16)

    def conv_cols(col0):
        cols = slice(col0, col0 + FF_CHUNK)
        h = _dot(hn, win_ref[:, cols])
        prev = carry_ref[:, cols]
        carry_ref[:, cols] = h[tm - SUBLANES:]
        w = cw_ref[:, cols]
        return (cb_ref[:, cols] + w[2:3] * h
                + w[0:1] * _shift_rows(h, prev, 2) + w[1:2] * _shift_rows(h, prev, 1))

    acc = jnp.zeros((tm, D_MODEL), F32)
    for c in range(D_FF // FF_CHUNK):
        gate = conv_cols(c * FF_CHUNK)
        up = conv_cols(D_FF + c * FF_CHUNK)
        act = (gate * (1.0 / (1.0 + jnp.exp(-gate))) * up).astype(BF16)
        acc = acc + _dot(act, wout_ref[c * FF_CHUNK:(c + 1) * FF_CHUNK, :])
    y = x + acc
    if final_norm:
        y = _rms(y, fn_ref[...])
    o_ref[0] = y


def _ffn(x, g, w_in, conv_w, conv_b, w_out, fn, tm, final_norm):
    batch, seq, _ = x.shape
    xspec = pl.BlockSpec((1, tm, D_MODEL), lambda b, s: (b, s, 0))
    return pl.pallas_call(
        functools.partial(_ffn_kernel, final_norm=final_norm),
        grid=(batch, seq // tm),
        in_specs=[
            xspec,
            _const_spec((1, D_MODEL)),
            _const_spec(w_in.shape),
            _const_spec(conv_w.shape),
            _const_spec(conv_b.shape),
            _const_spec(w_out.shape),
            _const_spec((1, D_MODEL)),
        ],
        out_specs=xspec,
        out_shape=jax.ShapeDtypeStruct(x.shape, F32),
        scratch_shapes=[pltpu.VMEM((SUBLANES, 2 * D_FF), F32)],
        compiler_params=_params(2),
        name="ffn_final" if final_norm else "ffn",
    )(x, g, w_in, conv_w, conv_b, w_out, fn)


def _split3(x):
    hi = x.astype(BF16)
    r = x - hi.astype(F32)
    mid = r.astype(BF16)
    lo = (r - mid.astype(F32)).astype(BF16)
    return hi, mid, lo


def _proj_b_kernel(x_ref, gkv_ref, wkv_ref, wf_ref, bf_ref, gq_ref, wq_ref,
                   k_ref, v_ref, ct_ref, cr_ref, qf_ref, qm_ref, carry_ref):
    @pl.when(pl.program_id(1) == 0)
    def _():
        carry_ref[...] = jnp.zeros_like(carry_ref)

    x = x_ref[0]
    tm = x.shape[0]
    xr = x * lax.rsqrt(jnp.mean(x * x, axis=-1, keepdims=True) + EPS)
    hkv = (xr * gkv_ref[...]).astype(BF16)
    kv = _dot(hkv, wkv_ref[...])
    k_ref[0] = kv[:, :TOK_WIDTH].astype(BF16)
    v_ref[0] = kv[:, TOK_WIDTH:].astype(BF16)

    f = _dot(hkv, wf_ref[...]) + bf_ref[...]
    log_f = jnp.minimum(f, 0.0) - jnp.log1p(jnp.exp(-jnp.abs(f)))
    tri = (lax.broadcasted_iota(jnp.int32, (tm, tm), 0)
           >= lax.broadcasted_iota(jnp.int32, (tm, tm), 1)).astype(BF16)
    hi, mid, lo = _split3(log_f)
    c = carry_ref[0:1, :] + ((_dot(tri, hi) + _dot(tri, mid)) + _dot(tri, lo))
    carry_ref[...] = jnp.broadcast_to(c[tm - 1:tm, :], carry_ref.shape)
    ct_ref[0] = c
    cr_ref[0] = c.T[:2 * SUBLANES, :]

    z = _dot((xr * gq_ref[...]).astype(BF16), wq_ref[...]) * Q_SCALE
    qf_ref[0] = z[:, :TOK_WIDTH].astype(BF16)
    qm_ref[0] = z[:, TOK_WIDTH:].astype(BF16)


def _proj_b(x, g_kv, w_kv, w_f, b_f, g_q, w_q, tm):
    batch, seq, _ = x.shape

    def tile(width):
        return pl.BlockSpec((1, tm, width), lambda b, s: (b, s, 0))

    return pl.pallas_call(
        _proj_b_kernel,
        grid=(batch, seq // tm),
        in_specs=[
            tile(D_MODEL),
            _const_spec((1, D_MODEL)),
            _const_spec(w_kv.shape),
            _const_spec(w_f.shape),
            _const_spec((1, LANES)),
            _const_spec((1, D_MODEL)),
            _const_spec(w_q.shape),
        ],
        out_specs=[
            tile(TOK_WIDTH), tile(TOK_WIDTH), tile(LANES),
            pl.BlockSpec((1, 2 * SUBLANES, tm), lambda b, s: (b, 0, s)),
            tile(TOK_WIDTH), tile(MEM_WIDTH),
        ],
        out_shape=[
            jax.ShapeDtypeStruct((batch, seq, TOK_WIDTH), BF16),
            jax.ShapeDtypeStruct((batch, seq, TOK_WIDTH), BF16),
            jax.ShapeDtypeStruct((batch, seq, LANES), F32),
            jax.ShapeDtypeStruct((batch, 2 * SUBLANES, seq), F32),
            jax.ShapeDtypeStruct((batch, seq, TOK_WIDTH), BF16),
            jax.ShapeDtypeStruct((batch, seq, MEM_WIDTH), BF16),
        ],
        scratch_shapes=[pltpu.VMEM((SUBLANES, LANES), F32)],
        compiler_params=_params(2),
        name="proj_b",
    )(x, g_kv, w_kv, w_f, b_f, g_q, w_q)


def _fox_kernel(q_ref, k_ref, v_ref, ct_ref, cr_ref, o_ref, *, tq):
    pair = pl.program_id(1)
    qi = pl.program_id(2)
    q2 = q_ref[0]
    ct_all = ct_ref[0]
    lane = lax.broadcasted_iota(jnp.int32, (tq, LANES), 1)
    low_half = lane < HEAD_DIM

    qs, cts = [], []
    for hh in range(2):
        qs.append(jnp.where(low_half if hh == 0 else ~low_half, q2, jnp.zeros_like(q2)))
        head = 2 * pair + hh
        cts.append(jnp.sum(jnp.where(lane == head, ct_all, 0.0), axis=-1, keepdims=True))

    def tile_update(j, carry, diagonal):
        k2 = k_ref[0, pl.ds(j * tq, tq), :]
        v2 = v_ref[0, pl.ds(j * tq, tq), :]
        new = []
        for hh in range(2):
            m, l, acc = carry[hh]
            cs = cr_ref[0, 0, hh, pl.ds(j, 1), :]
            s = _dot_nt(qs[hh], k2) + (cts[hh] - cs)
            if diagonal:
                keep = (lax.broadcasted_iota(jnp.int32, (tq, tq), 0)
                        >= lax.broadcasted_iota(jnp.int32, (tq, tq), 1))
                s = jnp.where(keep, s, -jnp.inf)
            m_new = jnp.maximum(m, jnp.max(s, axis=-1, keepdims=True))
            alpha = jnp.exp(m - m_new)
            p = jnp.exp(s - m_new)
            l = alpha * l + jnp.sum(p, axis=-1, keepdims=True)
            acc = alpha * acc + _dot(p.astype(BF16), v2)
            new.append((m_new, l, acc))
        return tuple(new)

    init = tuple((jnp.full((tq, 1), -jnp.inf, F32), jnp.zeros((tq, 1), F32),
                  jnp.zeros((tq, LANES), F32)) for _ in range(2))
    carry = lax.fori_loop(0, qi, lambda j, c: tile_update(j, c, False), init)
    (_, l0, a0), (_, l1, a1) = tile_update(qi, carry, True)
    o_ref[0] = jnp.where(low_half, a0 / l0, a1 / l1).astype(BF16)


def _fox_attention(q, k, v, c_tok, c_rows, tq):
    batch, seq, _ = q.shape
    n_pairs = TOK_WIDTH // HEAD_PAIR
    qspec = pl.BlockSpec((1, tq, HEAD_PAIR), lambda b, p, i: (b, i, p))
    kvspec = pl.BlockSpec((1, seq, HEAD_PAIR), lambda b, p, i: (b, 0, p))
    return pl.pallas_call(
        functools.partial(_fox_kernel, tq=tq),
        grid=(batch, n_pairs, seq // tq),
        in_specs=[
            qspec, kvspec, kvspec,
            pl.BlockSpec((1, tq, LANES), lambda b, p, i: (b, i, 0)),
            pl.BlockSpec((1, 1, 2, seq // tq, tq), lambda b, p, i: (b, p, 0, 0, 0)),
        ],
        out_specs=qspec,
        out_shape=jax.ShapeDtypeStruct((batch, seq, TOK_WIDTH), BF16),
        compiler_params=_params(3),
        name="fox_attention",
    )(q, k, v, c_tok, c_rows)


def _mixer_b_out_kernel(x_ref, tok_ref, qm_ref, kbd_ref, vbd_ref, wout_ref, o_ref):
    mem_o = _mem_attention(qm_ref[0], kbd_ref, vbd_ref).astype(BF16)
    cat = jnp.concatenate([tok_ref[0], mem_o], axis=-1)
    o_ref[0] = x_ref[0] + _dot(cat, wout_ref[...])


def _mixer_b_out(x, tok, q_mem, kvbd, layer, w_out, tm):
    batch, seq, _ = x.shape

    def tile(width):
        return pl.BlockSpec((1, tm, width), lambda b, s: (b, s, 0))

    kvspec = pl.BlockSpec((1, 1, N_MEM_HEADS * N_MEM, MEM_WIDTH), lambda b, s: (layer, b, 0, 0))
    return pl.pallas_call(
        _mixer_b_out_kernel,
        grid=(batch, seq // tm),
        in_specs=[tile(D_MODEL), tile(TOK_WIDTH), tile(MEM_WIDTH), kvspec, kvspec,
                  _const_spec(w_out.shape)],
        out_specs=tile(D_MODEL),
        out_shape=jax.ShapeDtypeStruct(x.shape, F32),
        compiler_params=_params(2),
        name="mixer_b_out",
    )(x, tok, q_mem, kvbd[0], kvbd[1], w_out)


def kernel(x, mem, a_norm1, a_w_in, a_v_norm, a_w_s, a_b_s, a_mem_norm, a_w_mem_kv, a_w_out, a_norm2, a_ffn_in, a_ffn_conv, a_ffn_conv_b, a_ffn_out, kv_norm, w_kv, b_f, b_norm1, b_w_q, b_mem_norm, b_w_mem_kv, b_w_out, b_norm2, b_ffn_in, b_ffn_conv, b_ffn_conv_b, b_ffn_out, final_norm):
    n_a, n_b = a_norm1.shape[0], b_norm1.shape[0]
    assert n_a == 1 and n_b == 1, "one gMLP layer, then one forgetting-attention layer"
    batch, seq, _ = x.shape
    tm = 256
    row = lambda a: a.reshape(1, -1)
    bf = lambda a: a.astype(BF16)

    kvbd = _memkv(mem,
                  jnp.concatenate([a_mem_norm, b_mem_norm], axis=0)[:, None, :],
                  bf(jnp.concatenate([a_w_mem_kv, b_w_mem_kv], axis=0)))

    b_full = jnp.repeat(a_b_s[0].T, TOK_WIDTH // GMLP_GROUPS, axis=1)
    x = _mixer_a(x, row(a_norm1[0]), bf(a_w_in[0]), row(a_v_norm[0]), a_w_s[0], b_full,
                 kvbd, 0, bf(a_w_out[0]), tm)
    x = _ffn(x, row(a_norm2[0]), bf(a_ffn_in[0]), a_ffn_conv[0], row(a_ffn_conv_b[0]),
             bf(a_ffn_out[0]), row(final_norm), tm, False)

    w_f = jnp.pad(w_kv[:, 2 * TOK_WIDTH:], ((0, 0), (0, LANES - N_FOX_HEADS)))
    b_f_pad = jnp.pad(b_f, (0, LANES - N_FOX_HEADS)).reshape(1, LANES)

    for j in range(n_b):
        k_sh, v_sh, c_tok, c_head, q_fox, q_mem = _proj_b(
            x, row(kv_norm), bf(w_kv[:, :2 * TOK_WIDTH]), bf(w_f), b_f_pad,
            row(b_norm1[j]), bf(b_w_q[j]), tm)
        c_rows = c_head[:, :N_FOX_HEADS, :].reshape(batch, N_FOX_HEADS // 2, 2, seq // tm, tm)
        tok = _fox_attention(q_fox, k_sh, v_sh, c_tok, c_rows, tm)
        x = _mixer_b_out(x, tok, q_mem, kvbd, n_a + j, bf(b_w_out[j]), tm)
        x = _ffn(x, row(b_norm2[j]), bf(b_ffn_in[j]), b_ffn_conv[j], row(b_ffn_conv_b[j]),
                 bf(b_ffn_out[j]), row(final_norm), tm, j == n_b - 1)
    return x
```

```python
import functools

import jax
import jax.numpy as jnp
from jax import lax
from jax.experimental import pallas as pl
from jax.experimental.pallas import tpu as pltpu

D_MODEL = 1024
N_MEM = 256
HEAD_DIM = 64
N_MEM_HEADS = 4
MEM_WIDTH = N_MEM_HEADS * HEAD_DIM
TOK_WIDTH = D_MODEL - MEM_WIDTH
N_FOX_HEADS = TOK_WIDTH // HEAD_DIM
GMLP_BLOCK = 128
GMLP_GROUPS = 4
D_FF = 2816
CONV_WIDTH = 3
EPS = 1e-6
Q_SCALE = HEAD_DIM ** -0.5

LANES = 128
SUBLANES = 8
HEAD_PAIR = 2 * HEAD_DIM
FF_CHUNK = 256
FOX_TQ = 512
VMEM_LIMIT = 56 * 1024 * 1024

BF16 = jnp.bfloat16
F32 = jnp.float32


def _const_spec(shape):
    nd = len(shape)
    return pl.BlockSpec(shape, lambda *_: (0,) * nd, pipeline_mode=pl.Buffered(1))


def _params(n_axes):
    return pltpu.CompilerParams(
        dimension_semantics=("arbitrary",) * n_axes, vmem_limit_bytes=VMEM_LIMIT)


def _rms(x, g):
    r = lax.rsqrt(jnp.mean(x * x, axis=-1, keepdims=True) + EPS)
    return x * r * g


def _dot(a, b):
    return jnp.dot(a, b, preferred_element_type=F32)


def _dot_nt(a, b):
    return lax.dot_general(a, b, (((1,), (1,)), ((), ())), preferred_element_type=F32)


def _memkv_kernel(mem_ref, g_ref, w_ref, kbd_ref, vbd_ref):
    mn = _rms(mem_ref[0], g_ref[0]).astype(BF16)
    kv = _dot(mn, w_ref[0])
    k, v = kv[:, :MEM_WIDTH], kv[:, MEM_WIDTH:]
    feat_head = lax.broadcasted_iota(jnp.int32, (N_MEM, MEM_WIDTH), 1) // HEAD_DIM
    for h in range(N_MEM_HEADS):
        rows = slice(h * N_MEM, (h + 1) * N_MEM)
        kbd_ref[0, 0, rows, :] = jnp.where(feat_head == h, k, 0.0).astype(BF16)
        vbd_ref[0, 0, rows, :] = jnp.where(feat_head == h, v, 0.0).astype(BF16)


def _memkv(mem, norms, ws):
    n_layers, batch = norms.shape[0], mem.shape[0]
    out = jax.ShapeDtypeStruct((n_layers, batch, N_MEM_HEADS * N_MEM, MEM_WIDTH), BF16)
    out_spec = pl.BlockSpec((1, 1, N_MEM_HEADS * N_MEM, MEM_WIDTH), lambda l, b: (l, b, 0, 0))
    return pl.pallas_call(
        _memkv_kernel,
        grid=(n_layers, batch),
        in_specs=[
            pl.BlockSpec((1, N_MEM, D_MODEL), lambda l, b: (b, 0, 0)),
            pl.BlockSpec((1, 1, D_MODEL), lambda l, b: (l, 0, 0)),
            pl.BlockSpec((1, D_MODEL, 2 * MEM_WIDTH), lambda l, b: (l, 0, 0)),
        ],
        out_specs=[out_spec, out_spec],
        out_shape=[out, out],
        compiler_params=_params(2),
        name="memkv",
    )(mem, norms, ws)


def _mem_attention(q, kbd_ref, vbd_ref):
    logits = _dot_nt(q, kbd_ref[0, 0])
    probs = []
    for h in range(N_MEM_HEADS):
        seg = logits[:, h * N_MEM:(h + 1) * N_MEM]
        e = jnp.exp(seg - jnp.max(seg, axis=-1, keepdims=True))
        probs.append((e / jnp.sum(e, axis=-1, keepdims=True)).astype(BF16))
    return _dot(jnp.concatenate(probs, axis=-1), vbd_ref[0, 0])


def _gmlp_mix(vn, ws_ref):
    tm = vn.shape[0]
    tri = (lax.broadcasted_iota(jnp.int32, (GMLP_BLOCK, GMLP_BLOCK), 0)
           >= lax.broadcasted_iota(jnp.int32, (GMLP_BLOCK, GMLP_BLOCK), 1))
    w = [jnp.where(tri, ws_ref[g], 0.0).astype(BF16) for g in range(GMLP_GROUPS)]
    first_half = lax.broadcasted_iota(jnp.int32, (GMLP_BLOCK, LANES), 1) < HEAD_DIM
    win_lo = (0, 128, 384, 512)
    blocks = []
    for r in range(tm // GMLP_BLOCK):
        rows = slice(r * GMLP_BLOCK, (r + 1) * GMLP_BLOCK)
        m = [_dot(w[g], vn[rows, win_lo[g]:win_lo[g] + 2 * LANES]) for g in range(GMLP_GROUPS)]
        blocks.append(jnp.concatenate([
            m[0][:, :LANES],
            jnp.where(first_half, m[0][:, LANES:], m[1][:, :LANES]),
            m[1][:, LANES:],
            m[2][:, :LANES],
            jnp.where(first_half, m[2][:, LANES:], m[3][:, :LANES]),
            m[3][:, LANES:],
        ], axis=-1))
    return jnp.concatenate(blocks, axis=0)


def _mixer_a_kernel(x_ref, g1_ref, win_ref, vg_ref, ws_ref, bs_ref, kbd_ref, vbd_ref, wout_ref,
                    o_ref):
    x = x_ref[0]
    tm = x.shape[0]
    h = _rms(x, g1_ref[...]).astype(BF16)
    z = _dot(h, win_ref[...])
    u = jax.nn.gelu(z[:, :TOK_WIDTH])
    v = jax.nn.gelu(z[:, TOK_WIDTH:2 * TOK_WIDTH])
    q_mem = (z[:, 2 * TOK_WIDTH:] * Q_SCALE).astype(BF16)
    vn = _rms(v, vg_ref[...]).astype(BF16)
    bias = jnp.concatenate([bs_ref[...]] * (tm // GMLP_BLOCK), axis=0)
    tok = (u * (_gmlp_mix(vn, ws_ref) + bias)).astype(BF16)
    mem_o = _mem_attention(q_mem, kbd_ref, vbd_ref).astype(BF16)
    cat = jnp.concatenate([tok, mem_o], axis=-1)
    o_ref[0] = x + _dot(cat, wout_ref[...])


def _mixer_a(x, g1, w_in, v_g, w_s, b_full, kvbd, layer, w_out, tm):
    batch, seq, _ = x.shape
    xspec = pl.BlockSpec((1, tm, D_MODEL), lambda b, s: (b, s, 0))
    kvspec = pl.BlockSpec((1, 1, N_MEM_HEADS * N_MEM, MEM_WIDTH), lambda b, s: (layer, b, 0, 0))
    return pl.pallas_call(
        _mixer_a_kernel,
        grid=(batch, seq // tm),
        in_specs=[
            xspec,
            _const_spec((1, D_MODEL)),
            _const_spec(w_in.shape),
            _const_spec((1, TOK_WIDTH)),
            _const_spec(w_s.shape),
            _const_spec(b_full.shape),
            kvspec, kvspec,
            _const_spec(w_out.shape),
        ],
        out_specs=xspec,
        out_shape=jax.ShapeDtypeStruct(x.shape, F32),
        compiler_params=_params(2),
        name="mixer_a",
    )(x, g1, w_in, v_g, w_s, b_full, kvbd[0], kvbd[1], w_out)


def _shift_rows(h, prev, n):
    rolled = pltpu.roll(h, n, 0)
    head = jnp.where(lax.broadcasted_iota(jnp.int32, prev.shape, 0) < n,
                     pltpu.roll(prev, n, 0), rolled[:SUBLANES])
    return jnp.concatenate([head, rolled[SUBLANES:]], axis=0)


def _ffn_kernel(x_ref, g_ref, win_ref, cw_ref, cb_ref, wout_ref, fn_ref, o_ref, carry_ref,
                *, final_norm):
    @pl.when(pl.program_id(1) == 0)
    def _():
        carry_ref[...] = jnp.zeros_like(carry_ref)

    x = x_ref[0]
    tm = x.shape[0]
    hn = _rms(x, g_ref[...]).astype(BF16)

    def conv_cols(col0):
        cols = slice(col0, col0 + FF_CHUNK)
        h = _dot(hn, win_ref[:, cols])
        prev = carry_ref[:, cols]
        carry_ref[:, cols] = h[tm - SUBLANES:]
        w = cw_ref[:, cols]
        return (cb_ref[:, cols] + w[2:3] * h
                + w[0:1] * _shift_rows(h, prev, 2) + w[1:2] * _shift_rows(h, prev, 1))

    acc = jnp.zeros((tm, D_MODEL), F32)
    for c in range(D_FF // FF_CHUNK):
        gate = conv_cols(c * FF_CHUNK)
        up = conv_cols(D_FF + c * FF_CHUNK)
        act = (gate * (1.0 / (1.0 + jnp.exp(-gate))) * up).astype(BF16)
        acc = acc + _dot(act, wout_ref[c * FF_CHUNK:(c + 1) * FF_CHUNK, :])
    y = x + acc
    if final_norm:
        y = _rms(y, fn_ref[...])
    o_ref[0] = y


def _ffn(x, g, w_in, conv_w, conv_b, w_out, fn, tm, final_norm):
    batch, seq, _ = x.shape
    xspec = pl.BlockSpec((1, tm, D_MODEL), lambda b, s: (b, s, 0))
    return pl.pallas_call(
        functools.partial(_ffn_kernel, final_norm=final_norm),
        grid=(batch, seq // tm),
        in_specs=[
            xspec,
            _const_spec((1, D_MODEL)),
            _const_spec(w_in.shape),
            _const_spec(conv_w.shape),
            _const_spec(conv_b.shape),
            _const_spec(w_out.shape),
            _const_spec((1, D_MODEL)),
        ],
        out_specs=xspec,
        out_shape=jax.ShapeDtypeStruct(x.shape, F32),
        scratch_shapes=[pltpu.VMEM((SUBLANES, 2 * D_FF), F32)],
        compiler_params=_params(2),
        name="ffn_final" if final_norm else "ffn",
    )(x, g, w_in, conv_w, conv_b, w_out, fn)


def _split3(x):
    hi = x.astype(BF16)
    r = x - hi.astype(F32)
    mid = r.astype(BF16)
    lo = (r - mid.astype(F32)).astype(BF16)
    return hi, mid, lo


def _proj_b_kernel(x_ref, gkv_ref, wkv_ref, wf_ref, bf_ref, gq_ref, wq_ref,
                   k_ref, v_ref, ct_ref, cr_ref, qf_ref, qm_ref, carry_ref):
    @pl.when(pl.program_id(1) == 0)
    def _():
        carry_ref[...] = jnp.zeros_like(carry_ref)

    x = x_ref[0]
    tm = x.shape[0]
    xr = x * lax.rsqrt(jnp.mean(x * x, axis=-1, keepdims=True) + EPS)
    hkv = (xr * gkv_ref[...]).astype(BF16)
    kv = _dot(hkv, wkv_ref[...])
    k_ref[0] = kv[:, :TOK_WIDTH].astype(BF16)
    v_ref[0] = kv[:, TOK_WIDTH:].astype(BF16)

    f = _dot(hkv, wf_ref[...]) + bf_ref[...]
    log_f = jnp.minimum(f, 0.0) - jnp.log1p(jnp.exp(-jnp.abs(f)))
    tri = (lax.broadcasted_iota(jnp.int32, (tm, tm), 0)
           >= lax.broadcasted_iota(jnp.int32, (tm, tm), 1)).astype(BF16)
    hi, mid, lo = _split3(log_f)
    c = carry_ref[0:1, :] + ((_dot(tri, hi) + _dot(tri, mid)) + _dot(tri, lo))
    carry_ref[...] = jnp.broadcast_to(c[tm - 1:tm, :], carry_ref.shape)
    ct_ref[0] = c
    cr_ref[0] = c.T[:2 * SUBLANES, :]

    z = _dot((xr * gq_ref[...]).astype(BF16), wq_ref[...]) * Q_SCALE
    qf_ref[0] = z[:, :TOK_WIDTH].astype(BF16)
    qm_ref[0] = z[:, TOK_WIDTH:].astype(BF16)


def _proj_b(x, g_kv, w_kv, w_f, b_f, g_q, w_q, tm):
    batch, seq, _ = x.shape

    def tile(width):
        return pl.BlockSpec((1, tm, width), lambda b, s: (b, s, 0))

    return pl.pallas_call(
        _proj_b_kernel,
        grid=(batch, seq // tm),
        in_specs=[
            tile(D_MODEL),
            _const_spec((1, D_MODEL)),
            _const_spec(w_kv.shape),
            _const_spec(w_f.shape),
            _const_spec((1, LANES)),
            _const_spec((1, D_MODEL)),
            _const_spec(w_q.shape),
        ],
        out_specs=[
            tile(TOK_WIDTH), tile(TOK_WIDTH), tile(LANES),
            pl.BlockSpec((1, 2 * SUBLANES, tm), lambda b, s: (b, 0, s)),
            tile(TOK_WIDTH), tile(MEM_WIDTH),
        ],
        out_shape=[
            jax.ShapeDtypeStruct((batch, seq, TOK_WIDTH), BF16),
            jax.ShapeDtypeStruct((batch, seq, TOK_WIDTH), BF16),
            jax.ShapeDtypeStruct((batch, seq, LANES), F32),
            jax.ShapeDtypeStruct((batch, 2 * SUBLANES, seq), F32),
            jax.ShapeDtypeStruct((batch, seq, TOK_WIDTH), BF16),
            jax.ShapeDtypeStruct((batch, seq, MEM_WIDTH), BF16),
        ],
        scratch_shapes=[pltpu.VMEM((SUBLANES, LANES), F32)],
        compiler_params=_params(2),
        name="proj_b",
    )(x, g_kv, w_kv, w_f, b_f, g_q, w_q)


def _fox_tile(n_keys, pair, q_ref, k_ref, v_ref, ct_ref, cr_ref, o_ref):
    tq = q_ref.shape[1]
    q2 = q_ref[0]
    k2 = k_ref[0, :n_keys, :]
    v2 = v_ref[0, :n_keys, :]
    ct_all = ct_ref[0]
    lane = lax.broadcasted_iota(jnp.int32, (tq, LANES), 1)
    low_half = lane < HEAD_DIM
    v_low = lax.broadcasted_iota(jnp.int32, v2.shape, 1) < HEAD_DIM
    keep = (lax.broadcasted_iota(jnp.int32, (tq, tq), 0)
            >= lax.broadcasted_iota(jnp.int32, (tq, tq), 1))
    outs = []
    for hh in range(2):
        mine = low_half if hh == 0 else ~low_half
        qh = jnp.where(mine, q2, jnp.zeros_like(q2))
        ct = jnp.sum(jnp.where(lane == 2 * pair + hh, ct_all, 0.0), axis=-1, keepdims=True)
        s = _dot_nt(qh, k2) - cr_ref[0, 0, hh:hh + 1, :n_keys]
        diag = jnp.where(keep, s[:, n_keys - tq:], -jnp.inf)
        s = diag if n_keys == tq else jnp.concatenate([s[:, :n_keys - tq], diag], axis=-1)
        m = jnp.max(s, axis=-1, keepdims=True) + ct
        p = jnp.exp(s + (ct - m)).astype(BF16)
        vh = jnp.where(v_low if hh == 0 else ~v_low, v2, jnp.ones_like(v2))
        pv = _dot(p, vh)
        outs.append(pv / pltpu.roll(pv, HEAD_DIM, 1))
    o_ref[0] = jnp.where(low_half, outs[0], outs[1]).astype(BF16)


def _fox_kernel(q_ref, k_ref, v_ref, ct_ref, cr_ref, o_ref):
    tq, seq = q_ref.shape[1], k_ref.shape[1]
    pair = pl.program_id(1)
    qi = pl.program_id(2)
    for i in range(seq // tq):
        pl.when(qi == i)(functools.partial(
            _fox_tile, (i + 1) * tq, pair, q_ref, k_ref, v_ref, ct_ref, cr_ref, o_ref))


def _fox_attention(q, k, v, c_tok, c_rows, tq):
    batch, seq, _ = q.shape
    n_pairs = TOK_WIDTH // HEAD_PAIR
    qspec = pl.BlockSpec((1, tq, HEAD_PAIR), lambda b, p, i: (b, i, p))
    kvspec = pl.BlockSpec((1, seq, HEAD_PAIR), lambda b, p, i: (b, 0, p))
    return pl.pallas_call(
        _fox_kernel,
        grid=(batch, n_pairs, seq // tq),
        in_specs=[
            qspec, kvspec, kvspec,
            pl.BlockSpec((1, tq, LANES), lambda b, p, i: (b, i, 0)),
            pl.BlockSpec((1, 1, 2, seq), lambda b, p, i: (b, p, 0, 0)),
        ],
        out_specs=qspec,
        out_shape=jax.ShapeDtypeStruct((batch, seq, TOK_WIDTH), BF16),
        compiler_params=_params(3),
        name="fox_attention",
    )(q, k, v, c_tok, c_rows)


def _mixer_b_out_kernel(x_ref, tok_ref, qm_ref, kbd_ref, vbd_ref, wout_ref, o_ref):
    mem_o = _mem_attention(qm_ref[0], kbd_ref, vbd_ref).astype(BF16)
    cat = jnp.concatenate([tok_ref[0], mem_o], axis=-1)
    o_ref[0] = x_ref[0] + _dot(cat, wout_ref[...])


def _mixer_b_out(x, tok, q_mem, kvbd, layer, w_out, tm):
    batch, seq, _ = x.shape

    def tile(width):
        return pl.BlockSpec((1, tm, width), lambda b, s: (b, s, 0))

    kvspec = pl.BlockSpec((1, 1, N_MEM_HEADS * N_MEM, MEM_WIDTH), lambda b, s: (layer, b, 0, 0))
    return pl.pallas_call(
        _mixer_b_out_kernel,
        grid=(batch, seq // tm),
        in_specs=[tile(D_MODEL), tile(TOK_WIDTH), tile(MEM_WIDTH), kvspec, kvspec,
                  _const_spec(w_out.shape)],
        out_specs=tile(D_MODEL),
        out_shape=jax.ShapeDtypeStruct(x.shape, F32),
        compiler_params=_params(2),
        name="mixer_b_out",
    )(x, tok, q_mem, kvbd[0], kvbd[1], w_out)


def kernel(x, mem, a_norm1, a_w_in, a_v_norm, a_w_s, a_b_s, a_mem_norm, a_w_mem_kv, a_w_out, a_norm2, a_ffn_in, a_ffn_conv, a_ffn_conv_b, a_ffn_out, kv_norm, w_kv, b_f, b_norm1, b_w_q, b_mem_norm, b_w_mem_kv, b_w_out, b_norm2, b_ffn_in, b_ffn_conv, b_ffn_conv_b, b_ffn_out, final_norm):
    n_a, n_b = a_norm1.shape[0], b_norm1.shape[0]
    assert n_a == 1 and n_b == 1, "one gMLP layer, then one forgetting-attention layer"
    batch, seq, _ = x.shape
    tm = 256
    row = lambda a: a.reshape(1, -1)
    bf = lambda a: a.astype(BF16)

    kvbd = _memkv(mem,
                  jnp.concatenate([a_mem_norm, b_mem_norm], axis=0)[:, None, :],
                  bf(jnp.concatenate([a_w_mem_kv, b_w_mem_kv], axis=0)))

    b_full = jnp.repeat(a_b_s[0].T, TOK_WIDTH // GMLP_GROUPS, axis=1)
    x = _mixer_a(x, row(a_norm1[0]), bf(a_w_in[0]), row(a_v_norm[0]), a_w_s[0], b_full,
                 kvbd, 0, bf(a_w_out[0]), tm)
    x = _ffn(x, row(a_norm2[0]), bf(a_ffn_in[0]), a_ffn_conv[0], row(a_ffn_conv_b[0]),
             bf(a_ffn_out[0]), row(final_norm), tm, False)

    w_f = jnp.pad(w_kv[:, 2 * TOK_WIDTH:], ((0, 0), (0, LANES - N_FOX_HEADS)))
    b_f_pad = jnp.pad(b_f, (0, LANES - N_FOX_HEADS)).reshape(1, LANES)

    for j in range(n_b):
        k_sh, v_sh, c_tok, c_head, q_fox, q_mem = _proj_b(
            x, row(kv_norm), bf(w_kv[:, :2 * TOK_WIDTH]), bf(w_f), b_f_pad,
            row(b_norm1[j]), bf(b_w_q[j]), tm)
        c_rows = c_head[:, :N_FOX_HEADS, :].reshape(batch, N_FOX_HEADS // 2, 2, seq)
        tok = _fox_attention(q_fox, k_sh, v_sh, c_tok, c_rows, FOX_TQ)
        x = _mixer_b_out(x, tok, q_mem, kvbd, n_a + j, bf(b_w_out[j]), tm)
        x = _ffn(x, row(b_norm2[j]), bf(b_ffn_in[j]), b_ffn_conv[j], row(b_ffn_conv_b[j]),
                 bf(b_ffn_out[j]), row(final_norm), tm, j == n_b - 1)
    return x
```

```python
import functools

import jax
import jax.numpy as jnp
from jax import lax
from jax.experimental import pallas as pl
from jax.experimental.pallas import tpu as pltpu

D_MODEL = 1024
N_MEM = 256
HEAD_DIM = 64
N_MEM_HEADS = 4
MEM_WIDTH = N_MEM_HEADS * HEAD_DIM
TOK_WIDTH = D_MODEL - MEM_WIDTH
N_FOX_HEADS = TOK_WIDTH // HEAD_DIM
GMLP_BLOCK = 128
GMLP_GROUPS = 4
D_FF = 2816
CONV_WIDTH = 3
EPS = 1e-6
Q_SCALE = HEAD_DIM ** -0.5

LANES = 128
SUBLANES = 8
HEAD_PAIR = 2 * HEAD_DIM
FF_CHUNK = 256
FOX_TQ = 512
FFN_TM = 512
ROW_TM = 256
VMEM_LIMIT = 56 * 1024 * 1024

BF16 = jnp.bfloat16
F32 = jnp.float32


def _const_spec(shape):
    nd = len(shape)
    return pl.BlockSpec(shape, lambda *_: (0,) * nd, pipeline_mode=pl.Buffered(1))


def _params(n_axes):
    return pltpu.CompilerParams(
        dimension_semantics=("arbitrary",) * n_axes, vmem_limit_bytes=VMEM_LIMIT)


def _rms(x, g):
    r = lax.rsqrt(jnp.mean(x * x, axis=-1, keepdims=True) + EPS)
    return x * r * g


def _dot(a, b):
    return jnp.dot(a, b, preferred_element_type=F32)


def _dot_nt(a, b):
    return lax.dot_general(a, b, (((1,), (1,)), ((), ())), preferred_element_type=F32)


def _memkv_kernel(mem_ref, g_ref, w_ref, kbd_ref, vbd_ref):
    mn = _rms(mem_ref[0], g_ref[0]).astype(BF16)
    kv = _dot(mn, w_ref[0])
    k, v = kv[:, :MEM_WIDTH], kv[:, MEM_WIDTH:]
    feat_head = lax.broadcasted_iota(jnp.int32, (N_MEM, MEM_WIDTH), 1) // HEAD_DIM
    for h in range(N_MEM_HEADS):
        rows = slice(h * N_MEM, (h + 1) * N_MEM)
        kbd_ref[0, 0, rows, :] = jnp.where(feat_head == h, k, 0.0).astype(BF16)
        vbd_ref[0, 0, rows, :] = jnp.where(feat_head == h, v, 0.0).astype(BF16)


def _memkv(mem, norms, ws):
    n_layers, batch = norms.shape[0], mem.shape[0]
    out = jax.ShapeDtypeStruct((n_layers, batch, N_MEM_HEADS * N_MEM, MEM_WIDTH), BF16)
    out_spec = pl.BlockSpec((1, 1, N_MEM_HEADS * N_MEM, MEM_WIDTH), lambda l, b: (l, b, 0, 0))
    return pl.pallas_call(
        _memkv_kernel,
        grid=(n_layers, batch),
        in_specs=[
            pl.BlockSpec((1, N_MEM, D_MODEL), lambda l, b: (b, 0, 0)),
            pl.BlockSpec((1, 1, D_MODEL), lambda l, b: (l, 0, 0)),
            pl.BlockSpec((1, D_MODEL, 2 * MEM_WIDTH), lambda l, b: (l, 0, 0)),
        ],
        out_specs=[out_spec, out_spec],
        out_shape=[out, out],
        compiler_params=_params(2),
        name="memkv",
    )(mem, norms, ws)


def _mem_attention(q, kbd_ref, vbd_ref):
    logits = _dot_nt(q, kbd_ref[0, 0])
    probs = []
    for h in range(N_MEM_HEADS):
        seg = logits[:, h * N_MEM:(h + 1) * N_MEM]
        e = jnp.exp(seg - jnp.max(seg, axis=-1, keepdims=True))
        probs.append((e / jnp.sum(e, axis=-1, keepdims=True)).astype(BF16))
    return _dot(jnp.concatenate(probs, axis=-1), vbd_ref[0, 0])


def _gmlp_mix(vn, ws_ref):
    tm = vn.shape[0]
    tri = (lax.broadcasted_iota(jnp.int32, (GMLP_BLOCK, GMLP_BLOCK), 0)
           >= lax.broadcasted_iota(jnp.int32, (GMLP_BLOCK, GMLP_BLOCK), 1))
    w = [jnp.where(tri, ws_ref[g], 0.0).astype(BF16) for g in range(GMLP_GROUPS)]
    first_half = lax.broadcasted_iota(jnp.int32, (GMLP_BLOCK, LANES), 1) < HEAD_DIM
    win_lo = (0, 128, 384, 512)
    blocks = []
    for r in range(tm // GMLP_BLOCK):
        rows = slice(r * GMLP_BLOCK, (r + 1) * GMLP_BLOCK)
        m = [_dot(w[g], vn[rows, win_lo[g]:win_lo[g] + 2 * LANES]) for g in range(GMLP_GROUPS)]
        blocks.append(jnp.concatenate([
            m[0][:, :LANES],
            jnp.where(first_half, m[0][:, LANES:], m[1][:, :LANES]),
            m[1][:, LANES:],
            m[2][:, :LANES],
            jnp.where(first_half, m[2][:, LANES:], m[3][:, :LANES]),
            m[3][:, LANES:],
        ], axis=-1))
    return jnp.concatenate(blocks, axis=0)


def _mixer_a_kernel(x_ref, g1_ref, win_ref, vg_ref, ws_ref, bs_ref, kbd_ref, vbd_ref, wout_ref,
                    o_ref):
    x = x_ref[0]
    tm = x.shape[0]
    h = _rms(x, g1_ref[...]).astype(BF16)
    z = _dot(h, win_ref[...])
    u = jax.nn.gelu(z[:, :TOK_WIDTH])
    v = jax.nn.gelu(z[:, TOK_WIDTH:2 * TOK_WIDTH])
    q_mem = (z[:, 2 * TOK_WIDTH:] * Q_SCALE).astype(BF16)
    vn = _rms(v, vg_ref[...]).astype(BF16)
    bias = jnp.concatenate([bs_ref[...]] * (tm // GMLP_BLOCK), axis=0)
    tok = (u * (_gmlp_mix(vn, ws_ref) + bias)).astype(BF16)
    mem_o = _mem_attention(q_mem, kbd_ref, vbd_ref).astype(BF16)
    cat = jnp.concatenate([tok, mem_o], axis=-1)
    o_ref[0] = x + _dot(cat, wout_ref[...])


def _mixer_a(x, g1, w_in, v_g, w_s, b_full, kvbd, layer, w_out, tm):
    batch, seq, _ = x.shape
    xspec = pl.BlockSpec((1, tm, D_MODEL), lambda b, s: (b, s, 0))
    kvspec = pl.BlockSpec((1, 1, N_MEM_HEADS * N_MEM, MEM_WIDTH), lambda b, s: (layer, b, 0, 0))
    return pl.pallas_call(
        _mixer_a_kernel,
        grid=(batch, seq // tm),
        in_specs=[
            xspec,
            _const_spec((1, D_MODEL)),
            _const_spec(w_in.shape),
            _const_spec((1, TOK_WIDTH)),
            _const_spec(w_s.shape),
            _const_spec(b_full.shape),
            kvspec, kvspec,
            _const_spec(w_out.shape),
        ],
        out_specs=xspec,
        out_shape=jax.ShapeDtypeStruct(x.shape, F32),
        compiler_params=_params(2),
        name="mixer_a",
    )(x, g1, w_in, v_g, w_s, b_full, kvbd[0], kvbd[1], w_out)


def _shift_rows(h, prev, n):
    rolled = pltpu.roll(h, n, 0)
    head = jnp.where(lax.broadcasted_iota(jnp.int32, prev.shape, 0) < n,
                     pltpu.roll(prev, n, 0), rolled[:SUBLANES])
    return jnp.concatenate([head, rolled[SUBLANES:]], axis=0)


def _ffn_kernel(x_ref, g_ref, win_ref, cw_ref, cb_ref, wout_ref, fn_ref, o_ref, carry_ref, act_ref,
                *, final_norm):
    @pl.when(pl.program_id(1) == 0)
    def _():
        carry_ref[...] = jnp.zeros_like(carry_ref)

    x = x_ref[0]
    tm = x.shape[0]
    hn = _rms(x, g_ref[...]).astype(BF16)

    def up_proj(col0):
        return _dot(hn, win_ref[:, col0:col0 + FF_CHUNK])

    def conv_cols(h, col0):
        cols = slice(col0, col0 + FF_CHUNK)
        prev = carry_ref[:, cols]
        carry_ref[:, cols] = h[tm - SUBLANES:]
        w = cw_ref[:, cols]
        return (cb_ref[:, cols] + w[2:3] * h
                + w[0:1] * _shift_rows(h, prev, 2) + w[1:2] * _shift_rows(h, prev, 1))

    for c in range(D_FF // FF_CHUNK):
        gate = conv_cols(up_proj(c * FF_CHUNK), c * FF_CHUNK)
        up = conv_cols(up_proj(D_FF + c * FF_CHUNK), D_FF + c * FF_CHUNK)
        act_ref[:, c * FF_CHUNK:(c + 1) * FF_CHUNK] = (
            gate * (1.0 / (1.0 + jnp.exp(-gate))) * up).astype(BF16)
    y = x + _dot(act_ref[...], wout_ref[...])
    if final_norm:
        y = _rms(y, fn_ref[...])
    o_ref[0] = y


def _ffn(x, g, w_in, conv_w, conv_b, w_out, fn, tm, final_norm):
    batch, seq, _ = x.shape
    xspec = pl.BlockSpec((1, tm, D_MODEL), lambda b, s: (b, s, 0))
    return pl.pallas_call(
        functools.partial(_ffn_kernel, final_norm=final_norm),
        grid=(batch, seq // tm),
        in_specs=[
            xspec,
            _const_spec((1, D_MODEL)),
            _const_spec(w_in.shape),
            _const_spec(conv_w.shape),
            _const_spec(conv_b.shape),
            _const_spec(w_out.shape),
            _const_spec((1, D_MODEL)),
        ],
        out_specs=xspec,
        out_shape=jax.ShapeDtypeStruct(x.shape, F32),
        scratch_shapes=[pltpu.VMEM((SUBLANES, 2 * D_FF), F32), pltpu.VMEM((tm, D_FF), BF16)],
        compiler_params=_params(2),
        name="ffn_final" if final_norm else "ffn",
    )(x, g, w_in, conv_w, conv_b, w_out, fn)


def _split3(x):
    hi = x.astype(BF16)
    r = x - hi.astype(F32)
    mid = r.astype(BF16)
    lo = (r - mid.astype(F32)).astype(BF16)
    return hi, mid, lo


def _proj_b_kernel(x_ref, gkv_ref, wkv_ref, wf_ref, bf_ref, gq_ref, wq_ref,
                   k_ref, v_ref, ct_ref, cr_ref, qf_ref, qm_ref, carry_ref):
    @pl.when(pl.program_id(1) == 0)
    def _():
        carry_ref[...] = jnp.zeros_like(carry_ref)

    x = x_ref[0]
    tm = x.shape[0]
    xr = x * lax.rsqrt(jnp.mean(x * x, axis=-1, keepdims=True) + EPS)
    hkv = (xr * gkv_ref[...]).astype(BF16)
    kv = _dot(hkv, wkv_ref[...])
    k_ref[0] = kv[:, :TOK_WIDTH].astype(BF16)
    v_ref[0] = kv[:, TOK_WIDTH:].astype(BF16)

    f = _dot(hkv, wf_ref[...]) + bf_ref[...]
    log_f = jnp.minimum(f, 0.0) - jnp.log1p(jnp.exp(-jnp.abs(f)))
    tri = (lax.broadcasted_iota(jnp.int32, (tm, tm), 0)
           >= lax.broadcasted_iota(jnp.int32, (tm, tm), 1)).astype(BF16)
    hi, mid, lo = _split3(log_f)
    c = carry_ref[0:1, :] + ((_dot(tri, hi) + _dot(tri, mid)) + _dot(tri, lo))
    carry_ref[...] = jnp.broadcast_to(c[tm - 1:tm, :], carry_ref.shape)
    ct_ref[0] = c
    cr_ref[0] = c.T[:2 * SUBLANES, :]

    z = _dot((xr * gq_ref[...]).astype(BF16), wq_ref[...]) * Q_SCALE
    qf_ref[0] = z[:, :TOK_WIDTH].astype(BF16)
    qm_ref[0] = z[:, TOK_WIDTH:].astype(BF16)


def _proj_b(x, g_kv, w_kv, w_f, b_f, g_q, w_q, tm):
    batch, seq, _ = x.shape

    def tile(width):
        return pl.BlockSpec((1, tm, width), lambda b, s: (b, s, 0))

    return pl.pallas_call(
        _proj_b_kernel,
        grid=(batch, seq // tm),
        in_specs=[
            tile(D_MODEL),
            _const_spec((1, D_MODEL)),
            _const_spec(w_kv.shape),
            _const_spec(w_f.shape),
            _const_spec((1, LANES)),
            _const_spec((1, D_MODEL)),
            _const_spec(w_q.shape),
        ],
        out_specs=[
            tile(TOK_WIDTH), tile(TOK_WIDTH), tile(LANES),
            pl.BlockSpec((1, 2 * SUBLANES, tm), lambda b, s: (b, 0, s)),
            tile(TOK_WIDTH), tile(MEM_WIDTH),
        ],
        out_shape=[
            jax.ShapeDtypeStruct((batch, seq, TOK_WIDTH), BF16),
            jax.ShapeDtypeStruct((batch, seq, TOK_WIDTH), BF16),
            jax.ShapeDtypeStruct((batch, seq, LANES), F32),
            jax.ShapeDtypeStruct((batch, 2 * SUBLANES, seq), F32),
            jax.ShapeDtypeStruct((batch, seq, TOK_WIDTH), BF16),
            jax.ShapeDtypeStruct((batch, seq, MEM_WIDTH), BF16),
        ],
        scratch_shapes=[pltpu.VMEM((SUBLANES, LANES), F32)],
        compiler_params=_params(2),
        name="proj_b",
    )(x, g_kv, w_kv, w_f, b_f, g_q, w_q)


def _fox_tile(n_keys, pair, q_ref, k_ref, v_ref, ct_ref, cr_ref, o_ref):
    tq = q_ref.shape[1]
    q2 = q_ref[0]
    k2 = k_ref[0, :n_keys, :]
    v2 = v_ref[0, :n_keys, :]
    ct_all = ct_ref[0]
    lane = lax.broadcasted_iota(jnp.int32, (tq, LANES), 1)
    low_half = lane < HEAD_DIM
    v_low = lax.broadcasted_iota(jnp.int32, v2.shape, 1) < HEAD_DIM
    keep = (lax.broadcasted_iota(jnp.int32, (tq, tq), 0)
            >= lax.broadcasted_iota(jnp.int32, (tq, tq), 1))
    outs = []
    for hh in range(2):
        mine = low_half if hh == 0 else ~low_half
        qh = jnp.where(mine, q2, jnp.zeros_like(q2))
        ct = jnp.sum(jnp.where(lane == 2 * pair + hh, ct_all, 0.0), axis=-1, keepdims=True)
        s = _dot_nt(qh, k2) - cr_ref[0, 0, hh:hh + 1, :n_keys]
        diag = jnp.where(keep, s[:, n_keys - tq:], -jnp.inf)
        s = diag if n_keys == tq else jnp.concatenate([s[:, :n_keys - tq], diag], axis=-1)
        m = jnp.max(s, axis=-1, keepdims=True) + ct
        p = jnp.exp(s + (ct - m)).astype(BF16)
        vh = jnp.where(v_low if hh == 0 else ~v_low, v2, jnp.ones_like(v2))
        pv = _dot(p, vh)
        outs.append(pv / pltpu.roll(pv, HEAD_DIM, 1))
    o_ref[0] = jnp.where(low_half, outs[0], outs[1]).astype(BF16)


def _fox_kernel(q_ref, k_ref, v_ref, ct_ref, cr_ref, o_ref):
    tq, seq = q_ref.shape[1], k_ref.shape[1]
    pair = pl.program_id(1)
    qi = pl.program_id(2)
    for i in range(seq // tq):
        pl.when(qi == i)(functools.partial(
            _fox_tile, (i + 1) * tq, pair, q_ref, k_ref, v_ref, ct_ref, cr_ref, o_ref))


def _fox_attention(q, k, v, c_tok, c_rows, tq):
    batch, seq, _ = q.shape
    n_pairs = TOK_WIDTH // HEAD_PAIR
    qspec = pl.BlockSpec((1, tq, HEAD_PAIR), lambda b, p, i: (b, i, p))
    kvspec = pl.BlockSpec((1, seq, HEAD_PAIR), lambda b, p, i: (b, 0, p))
    return pl.pallas_call(
        _fox_kernel,
        grid=(batch, n_pairs, seq // tq),
        in_specs=[
            qspec, kvspec, kvspec,
            pl.BlockSpec((1, tq, LANES), lambda b, p, i: (b, i, 0)),
            pl.BlockSpec((1, 1, 2, seq), lambda b, p, i: (b, p, 0, 0)),
        ],
        out_specs=qspec,
        out_shape=jax.ShapeDtypeStruct((batch, seq, TOK_WIDTH), BF16),
        compiler_params=_params(3),
        name="fox_attention",
    )(q, k, v, c_tok, c_rows)


def _mixer_b_out_kernel(x_ref, tok_ref, qm_ref, kbd_ref, vbd_ref, wout_ref, o_ref):
    mem_o = _mem_attention(qm_ref[0], kbd_ref, vbd_ref).astype(BF16)
    cat = jnp.concatenate([tok_ref[0], mem_o], axis=-1)
    o_ref[0] = x_ref[0] + _dot(cat, wout_ref[...])


def _mixer_b_out(x, tok, q_mem, kvbd, layer, w_out, tm):
    batch, seq, _ = x.shape

    def tile(width):
        return pl.BlockSpec((1, tm, width), lambda b, s: (b, s, 0))

    kvspec = pl.BlockSpec((1, 1, N_MEM_HEADS * N_MEM, MEM_WIDTH), lambda b, s: (layer, b, 0, 0))
    return pl.pallas_call(
        _mixer_b_out_kernel,
        grid=(batch, seq // tm),
        in_specs=[tile(D_MODEL), tile(TOK_WIDTH), tile(MEM_WIDTH), kvspec, kvspec,
                  _const_spec(w_out.shape)],
        out_specs=tile(D_MODEL),
        out_shape=jax.ShapeDtypeStruct(x.shape, F32),
        compiler_params=_params(2),
        name="mixer_b_out",
    )(x, tok, q_mem, kvbd[0], kvbd[1], w_out)


def kernel(x, mem, a_norm1, a_w_in, a_v_norm, a_w_s, a_b_s, a_mem_norm, a_w_mem_kv, a_w_out, a_norm2, a_ffn_in, a_ffn_conv, a_ffn_conv_b, a_ffn_out, kv_norm, w_kv, b_f, b_norm1, b_w_q, b_mem_norm, b_w_mem_kv, b_w_out, b_norm2, b_ffn_in, b_ffn_conv, b_ffn_conv_b, b_ffn_out, final_norm):
    n_a, n_b = a_norm1.shape[0], b_norm1.shape[0]
    assert n_a == 1 and n_b == 1, "one gMLP layer, then one forgetting-attention layer"
    batch, seq, _ = x.shape
    tm = ROW_TM
    row = lambda a: a.reshape(1, -1)
    bf = lambda a: a.astype(BF16)

    kvbd = _memkv(mem,
                  jnp.concatenate([a_mem_norm, b_mem_norm], axis=0)[:, None, :],
                  bf(jnp.concatenate([a_w_mem_kv, b_w_mem_kv], axis=0)))

    b_full = jnp.repeat(a_b_s[0].T, TOK_WIDTH // GMLP_GROUPS, axis=1)
    x = _mixer_a(x, row(a_norm1[0]), bf(a_w_in[0]), row(a_v_norm[0]), a_w_s[0], b_full,
                 kvbd, 0, bf(a_w_out[0]), tm)
    x = _ffn(x, row(a_norm2[0]), bf(a_ffn_in[0]), a_ffn_conv[0], row(a_ffn_conv_b[0]),
             bf(a_ffn_out[0]), row(final_norm), FFN_TM, False)

    w_f = jnp.pad(w_kv[:, 2 * TOK_WIDTH:], ((0, 0), (0, LANES - N_FOX_HEADS)))
    b_f_pad = jnp.pad(b_f, (0, LANES - N_FOX_HEADS)).reshape(1, LANES)

    for j in range(n_b):
        k_sh, v_sh, c_tok, c_head, q_fox, q_mem = _proj_b(
            x, row(kv_norm), bf(w_kv[:, :2 * TOK_WIDTH]), bf(w_f), b_f_pad,
            row(b_norm1[j]), bf(b_w_q[j]), tm)
        c_rows = c_head[:, :N_FOX_HEADS, :].reshape(batch, N_FOX_HEADS // 2, 2, seq)
        tok = _fox_attention(q_fox, k_sh, v_sh, c_tok, c_rows, FOX_TQ)
        x = _mixer_b_out(x, tok, q_mem, kvbd, n_a + j, bf(b_w_out[j]), tm)
        x = _ffn(x, row(b_norm2[j]), bf(b_ffn_in[j]), b_ffn_conv[j], row(b_ffn_conv_b[j]),
                 bf(b_ffn_out[j]), row(final_norm), FFN_TM, j == n_b - 1)
    return x
```

```python
import functools

import jax
import jax.numpy as jnp
from jax import lax
from jax.experimental import pallas as pl
from jax.experimental.pallas import tpu as pltpu

D_MODEL = 1024
N_MEM = 256
HEAD_DIM = 64
N_MEM_HEADS = 4
MEM_WIDTH = N_MEM_HEADS * HEAD_DIM
TOK_WIDTH = D_MODEL - MEM_WIDTH
N_FOX_HEADS = TOK_WIDTH // HEAD_DIM
GMLP_BLOCK = 128
GMLP_GROUPS = 4
D_FF = 2816
CONV_WIDTH = 3
EPS = 1e-6
Q_SCALE = HEAD_DIM ** -0.5

LANES = 128
SUBLANES = 8
HEAD_PAIR = 2 * HEAD_DIM
FF_CHUNK = 256
FOX_TQ = 512
FOX_TK = 512
FFN_TM = 512
ROW_TM = 256
VMEM_LIMIT = 56 * 1024 * 1024

BF16 = jnp.bfloat16
F32 = jnp.float32


def _const_spec(shape):
    nd = len(shape)
    return pl.BlockSpec(shape, lambda *_: (0,) * nd, pipeline_mode=pl.Buffered(1))


def _params(n_axes):
    return pltpu.CompilerParams(
        dimension_semantics=("arbitrary",) * n_axes, vmem_limit_bytes=VMEM_LIMIT)


def _rms(x, g):
    r = lax.rsqrt(jnp.mean(x * x, axis=-1, keepdims=True) + EPS)
    return x * r * g


def _dot(a, b):
    return jnp.dot(a, b, preferred_element_type=F32)


def _dot_nt(a, b):
    return lax.dot_general(a, b, (((1,), (1,)), ((), ())), preferred_element_type=F32)


def _memkv_kernel(mem_ref, g_ref, w_ref, kbd_ref, vbd_ref):
    mn = _rms(mem_ref[0], g_ref[0]).astype(BF16)
    kv = _dot(mn, w_ref[0])
    k, v = kv[:, :MEM_WIDTH], kv[:, MEM_WIDTH:]
    feat_head = lax.broadcasted_iota(jnp.int32, (N_MEM, MEM_WIDTH), 1) // HEAD_DIM
    for h in range(N_MEM_HEADS):
        rows = slice(h * N_MEM, (h + 1) * N_MEM)
        kbd_ref[0, 0, rows, :] = jnp.where(feat_head == h, k, 0.0).astype(BF16)
        vbd_ref[0, 0, rows, :] = jnp.where(feat_head == h, v, 0.0).astype(BF16)


def _memkv(mem, norms, ws):
    n_layers, batch = norms.shape[0], mem.shape[0]
    out = jax.ShapeDtypeStruct((n_layers, batch, N_MEM_HEADS * N_MEM, MEM_WIDTH), BF16)
    out_spec = pl.BlockSpec((1, 1, N_MEM_HEADS * N_MEM, MEM_WIDTH), lambda l, b: (l, b, 0, 0))
    return pl.pallas_call(
        _memkv_kernel,
        grid=(n_layers, batch),
        in_specs=[
            pl.BlockSpec((1, N_MEM, D_MODEL), lambda l, b: (b, 0, 0)),
            pl.BlockSpec((1, 1, D_MODEL), lambda l, b: (l, 0, 0)),
            pl.BlockSpec((1, D_MODEL, 2 * MEM_WIDTH), lambda l, b: (l, 0, 0)),
        ],
        out_specs=[out_spec, out_spec],
        out_shape=[out, out],
        compiler_params=_params(2),
        name="memkv",
    )(mem, norms, ws)


def _mem_attention(q, kbd_ref, vbd_ref):
    logits = _dot_nt(q, kbd_ref[0, 0])
    probs = []
    for h in range(N_MEM_HEADS):
        seg = logits[:, h * N_MEM:(h + 1) * N_MEM]
        e = jnp.exp(seg - jnp.max(seg, axis=-1, keepdims=True))
        probs.append((e / jnp.sum(e, axis=-1, keepdims=True)).astype(BF16))
    return _dot(jnp.concatenate(probs, axis=-1), vbd_ref[0, 0])


def _gmlp_mix(vn, ws_ref):
    tm = vn.shape[0]
    tri = (lax.broadcasted_iota(jnp.int32, (GMLP_BLOCK, GMLP_BLOCK), 0)
           >= lax.broadcasted_iota(jnp.int32, (GMLP_BLOCK, GMLP_BLOCK), 1))
    w = [jnp.where(tri, ws_ref[g], 0.0).astype(BF16) for g in range(GMLP_GROUPS)]
    first_half = lax.broadcasted_iota(jnp.int32, (GMLP_BLOCK, LANES), 1) < HEAD_DIM
    win_lo = (0, 128, 384, 512)
    blocks = []
    for r in range(tm // GMLP_BLOCK):
        rows = slice(r * GMLP_BLOCK, (r + 1) * GMLP_BLOCK)
        m = [_dot(w[g], vn[rows, win_lo[g]:win_lo[g] + 2 * LANES]) for g in range(GMLP_GROUPS)]
        blocks.append(jnp.concatenate([
            m[0][:, :LANES],
            jnp.where(first_half, m[0][:, LANES:], m[1][:, :LANES]),
            m[1][:, LANES:],
            m[2][:, :LANES],
            jnp.where(first_half, m[2][:, LANES:], m[3][:, :LANES]),
            m[3][:, LANES:],
        ], axis=-1))
    return jnp.concatenate(blocks, axis=0)


def _mixer_a_kernel(x_ref, g1_ref, win_ref, vg_ref, ws_ref, bs_ref, kbd_ref, vbd_ref, wout_ref,
                    o_ref):
    x = x_ref[0]
    tm = x.shape[0]
    h = _rms(x, g1_ref[...]).astype(BF16)
    z = _dot(h, win_ref[...])
    u = jax.nn.gelu(z[:, :TOK_WIDTH])
    v = jax.nn.gelu(z[:, TOK_WIDTH:2 * TOK_WIDTH])
    q_mem = (z[:, 2 * TOK_WIDTH:] * Q_SCALE).astype(BF16)
    vn = _rms(v, vg_ref[...]).astype(BF16)
    bias = jnp.concatenate([bs_ref[...]] * (tm // GMLP_BLOCK), axis=0)
    tok = (u * (_gmlp_mix(vn, ws_ref) + bias)).astype(BF16)
    mem_o = _mem_attention(q_mem, kbd_ref, vbd_ref).astype(BF16)
    cat = jnp.concatenate([tok, mem_o], axis=-1)
    o_ref[0] = x + _dot(cat, wout_ref[...])


def _mixer_a(x, g1, w_in, v_g, w_s, b_full, kvbd, layer, w_out, tm):
    batch, seq, _ = x.shape
    xspec = pl.BlockSpec((1, tm, D_MODEL), lambda b, s: (b, s, 0))
    kvspec = pl.BlockSpec((1, 1, N_MEM_HEADS * N_MEM, MEM_WIDTH), lambda b, s: (layer, b, 0, 0))
    return pl.pallas_call(
        _mixer_a_kernel,
        grid=(batch, seq // tm),
        in_specs=[
            xspec,
            _const_spec((1, D_MODEL)),
            _const_spec(w_in.shape),
            _const_spec((1, TOK_WIDTH)),
            _const_spec(w_s.shape),
            _const_spec(b_full.shape),
            kvspec, kvspec,
            _const_spec(w_out.shape),
        ],
        out_specs=xspec,
        out_shape=jax.ShapeDtypeStruct(x.shape, F32),
        compiler_params=_params(2),
        name="mixer_a",
    )(x, g1, w_in, v_g, w_s, b_full, kvbd[0], kvbd[1], w_out)


def _shift_rows(h, prev, n):
    rolled = pltpu.roll(h, n, 0)
    head = jnp.where(lax.broadcasted_iota(jnp.int32, prev.shape, 0) < n,
                     pltpu.roll(prev, n, 0), rolled[:SUBLANES])
    return jnp.concatenate([head, rolled[SUBLANES:]], axis=0)


def _ffn_kernel(x_ref, g_ref, win_ref, cw_ref, cb_ref, wout_ref, fn_ref, o_ref, carry_ref, act_ref,
                *, final_norm):
    @pl.when(pl.program_id(1) == 0)
    def _():
        carry_ref[...] = jnp.zeros_like(carry_ref)

    x = x_ref[0]
    tm = x.shape[0]
    hn = _rms(x, g_ref[...]).astype(BF16)

    def up_proj(col0):
        return _dot(hn, win_ref[:, col0:col0 + FF_CHUNK])

    def conv_cols(h, col0):
        cols = slice(col0, col0 + FF_CHUNK)
        prev = carry_ref[:, cols]
        carry_ref[:, cols] = h[tm - SUBLANES:]
        w = cw_ref[:, cols]
        return (cb_ref[:, cols] + w[2:3] * h
                + w[0:1] * _shift_rows(h, prev, 2) + w[1:2] * _shift_rows(h, prev, 1))

    for c in range(D_FF // FF_CHUNK):
        gate = conv_cols(up_proj(c * FF_CHUNK), c * FF_CHUNK)
        up = conv_cols(up_proj(D_FF + c * FF_CHUNK), D_FF + c * FF_CHUNK)
        act_ref[:, c * FF_CHUNK:(c + 1) * FF_CHUNK] = (
            gate * (1.0 / (1.0 + jnp.exp(-gate))) * up).astype(BF16)
    y = x + _dot(act_ref[...], wout_ref[...])
    if final_norm:
        y = _rms(y, fn_ref[...])
    o_ref[0] = y


def _ffn(x, g, w_in, conv_w, conv_b, w_out, fn, tm, final_norm):
    batch, seq, _ = x.shape
    xspec = pl.BlockSpec((1, tm, D_MODEL), lambda b, s: (b, s, 0))
    return pl.pallas_call(
        functools.partial(_ffn_kernel, final_norm=final_norm),
        grid=(batch, seq // tm),
        in_specs=[
            xspec,
            _const_spec((1, D_MODEL)),
            _const_spec(w_in.shape),
            _const_spec(conv_w.shape),
            _const_spec(conv_b.shape),
            _const_spec(w_out.shape),
            _const_spec((1, D_MODEL)),
        ],
        out_specs=xspec,
        out_shape=jax.ShapeDtypeStruct(x.shape, F32),
        scratch_shapes=[pltpu.VMEM((SUBLANES, 2 * D_FF), F32), pltpu.VMEM((tm, D_FF), BF16)],
        compiler_params=_params(2),
        name="ffn_final" if final_norm else "ffn",
    )(x, g, w_in, conv_w, conv_b, w_out, fn)


def _split3(x):
    hi = x.astype(BF16)
    r = x - hi.astype(F32)
    mid = r.astype(BF16)
    lo = (r - mid.astype(F32)).astype(BF16)
    return hi, mid, lo


def _proj_b_kernel(x_ref, gkv_ref, wkv_ref, wf_ref, bf_ref, gq_ref, wq_ref,
                   k_ref, v_ref, ct_ref, cr_ref, qf_ref, qm_ref, carry_ref):
    @pl.when(pl.program_id(1) == 0)
    def _():
        carry_ref[...] = jnp.zeros_like(carry_ref)

    x = x_ref[0]
    tm = x.shape[0]
    xr = x * lax.rsqrt(jnp.mean(x * x, axis=-1, keepdims=True) + EPS)
    hkv = (xr * gkv_ref[...]).astype(BF16)
    kv = _dot(hkv, wkv_ref[...])
    k_ref[0] = kv[:, :TOK_WIDTH].astype(BF16)
    v_ref[0] = kv[:, TOK_WIDTH:].astype(BF16)

    f = _dot(hkv, wf_ref[...]) + bf_ref[...]
    log_f = jnp.minimum(f, 0.0) - jnp.log1p(jnp.exp(-jnp.abs(f)))
    tri = (lax.broadcasted_iota(jnp.int32, (tm, tm), 0)
           >= lax.broadcasted_iota(jnp.int32, (tm, tm), 1)).astype(BF16)
    hi, mid, lo = _split3(log_f)
    c = carry_ref[0:1, :] + ((_dot(tri, hi) + _dot(tri, mid)) + _dot(tri, lo))
    carry_ref[...] = jnp.broadcast_to(c[tm - 1:tm, :], carry_ref.shape)
    ct_ref[0] = c
    cr_ref[0] = c.T[:2 * SUBLANES, :]

    z = _dot((xr * gq_ref[...]).astype(BF16), wq_ref[...]) * Q_SCALE
    qf_ref[0] = z[:, :TOK_WIDTH].astype(BF16)
    qm_ref[0] = z[:, TOK_WIDTH:].astype(BF16)


def _proj_b(x, g_kv, w_kv, w_f, b_f, g_q, w_q, tm):
    batch, seq, _ = x.shape

    def tile(width):
        return pl.BlockSpec((1, tm, width), lambda b, s: (b, s, 0))

    return pl.pallas_call(
        _proj_b_kernel,
        grid=(batch, seq // tm),
        in_specs=[
            tile(D_MODEL),
            _const_spec((1, D_MODEL)),
            _const_spec(w_kv.shape),
            _const_spec(w_f.shape),
            _const_spec((1, LANES)),
            _const_spec((1, D_MODEL)),
            _const_spec(w_q.shape),
        ],
        out_specs=[
            tile(TOK_WIDTH), tile(TOK_WIDTH), tile(LANES),
            pl.BlockSpec((1, 2 * SUBLANES, tm), lambda b, s: (b, 0, s)),
            tile(TOK_WIDTH), tile(MEM_WIDTH),
        ],
        out_shape=[
            jax.ShapeDtypeStruct((batch, seq, TOK_WIDTH), BF16),
            jax.ShapeDtypeStruct((batch, seq, TOK_WIDTH), BF16),
            jax.ShapeDtypeStruct((batch, seq, LANES), F32),
            jax.ShapeDtypeStruct((batch, 2 * SUBLANES, seq), F32),
            jax.ShapeDtypeStruct((batch, seq, TOK_WIDTH), BF16),
            jax.ShapeDtypeStruct((batch, seq, MEM_WIDTH), BF16),
        ],
        scratch_shapes=[pltpu.VMEM((SUBLANES, LANES), F32)],
        compiler_params=_params(2),
        name="proj_b",
    )(x, g_kv, w_kv, w_f, b_f, g_q, w_q)


def _fox_tile(n_keys, pair, q_ref, k_ref, v_ref, ct_ref, cr_ref, o_ref):
    tq, tk = q_ref.shape[1], FOX_TK
    n_tiles = n_keys // tk
    q2 = q_ref[0]
    ct_all = ct_ref[0]
    lane = lax.broadcasted_iota(jnp.int32, (tq, LANES), 1)
    low_half = lane < HEAD_DIM
    v_low = lax.broadcasted_iota(jnp.int32, (tk, LANES), 1) < HEAD_DIM
    q_pos = lax.broadcasted_iota(jnp.int32, (tq, tk), 0) + (n_keys - tq)
    k_pos = lax.broadcasted_iota(jnp.int32, (tq, tk), 1)
    heads = (0, 1)
    qh = [jnp.where(low_half if hh == 0 else ~low_half, q2, jnp.zeros_like(q2)) for hh in heads]
    ct = [jnp.sum(jnp.where(lane == 2 * pair + hh, ct_all, 0.0), axis=-1, keepdims=True)
          for hh in heads]

    s = [[None] * n_tiles for _ in heads]
    m_rows = [None, None]
    for t in range(n_tiles):
        keys = slice(t * tk, (t + 1) * tk)
        k_t = k_ref[0, keys, :]
        for hh in heads:
            s_t = _dot_nt(qh[hh], k_t) - cr_ref[0, 0, hh:hh + 1, keys]
            if (t + 1) * tk > n_keys - tq:
                s_t = jnp.where(q_pos >= k_pos + t * tk, s_t, -jnp.inf)
            s[hh][t] = s_t
            m_t = jnp.max(s_t, axis=-1, keepdims=True)
            m_rows[hh] = m_t if t == 0 else jnp.maximum(m_rows[hh], m_t)

    shift = []
    for hh in heads:
        shift.append(ct[hh] - (m_rows[hh] + ct[hh]))
    pv = [None, None]
    for t in range(n_tiles):
        v_t = v_ref[0, t * tk:(t + 1) * tk, :]
        for hh in heads:
            p_t = jnp.exp(s[hh][t] + shift[hh]).astype(BF16)
            v_h = jnp.where(v_low if hh == 0 else ~v_low, v_t, jnp.ones_like(v_t))
            pv_t = _dot(p_t, v_h)
            pv[hh] = pv_t if t == 0 else pv[hh] + pv_t
    outs = [pv[hh] / pltpu.roll(pv[hh], HEAD_DIM, 1) for hh in heads]
    o_ref[0] = jnp.where(low_half, outs[0], outs[1]).astype(BF16)


def _fox_kernel(q_ref, k_ref, v_ref, ct_ref, cr_ref, o_ref):
    tq, seq = q_ref.shape[1], k_ref.shape[1]
    pair = pl.program_id(1)
    qi = pl.program_id(2)
    for i in range(seq // tq):
        pl.when(qi == i)(functools.partial(
            _fox_tile, (i + 1) * tq, pair, q_ref, k_ref, v_ref, ct_ref, cr_ref, o_ref))


def _fox_attention(q, k, v, c_tok, c_rows, tq):
    batch, seq, _ = q.shape
    n_pairs = TOK_WIDTH // HEAD_PAIR
    qspec = pl.BlockSpec((1, tq, HEAD_PAIR), lambda b, p, i: (b, i, p))
    kvspec = pl.BlockSpec((1, seq, HEAD_PAIR), lambda b, p, i: (b, 0, p))
    return pl.pallas_call(
        _fox_kernel,
        grid=(batch, n_pairs, seq // tq),
        in_specs=[
            qspec, kvspec, kvspec,
            pl.BlockSpec((1, tq, LANES), lambda b, p, i: (b, i, 0)),
            pl.BlockSpec((1, 1, 2, seq), lambda b, p, i: (b, p, 0, 0)),
        ],
        out_specs=qspec,
        out_shape=jax.ShapeDtypeStruct((batch, seq, TOK_WIDTH), BF16),
        compiler_params=_params(3),
        name="fox_attention",
    )(q, k, v, c_tok, c_rows)


def _mixer_b_out_kernel(x_ref, tok_ref, qm_ref, kbd_ref, vbd_ref, wout_ref, o_ref):
    mem_o = _mem_attention(qm_ref[0], kbd_ref, vbd_ref).astype(BF16)
    cat = jnp.concatenate([tok_ref[0], mem_o], axis=-1)
    o_ref[0] = x_ref[0] + _dot(cat, wout_ref[...])


def _mixer_b_out(x, tok, q_mem, kvbd, layer, w_out, tm):
    batch, seq, _ = x.shape

    def tile(width):
        return pl.BlockSpec((1, tm, width), lambda b, s: (b, s, 0))

    kvspec = pl.BlockSpec((1, 1, N_MEM_HEADS * N_MEM, MEM_WIDTH), lambda b, s: (layer, b, 0, 0))
    return pl.pallas_call(
        _mixer_b_out_kernel,
        grid=(batch, seq // tm),
        in_specs=[tile(D_MODEL), tile(TOK_WIDTH), tile(MEM_WIDTH), kvspec, kvspec,
                  _const_spec(w_out.shape)],
        out_specs=tile(D_MODEL),
        out_shape=jax.ShapeDtypeStruct(x.shape, F32),
        compiler_params=_params(2),
        name="mixer_b_out",
    )(x, tok, q_mem, kvbd[0], kvbd[1], w_out)


def kernel(x, mem, a_norm1, a_w_in, a_v_norm, a_w_s, a_b_s, a_mem_norm, a_w_mem_kv, a_w_out, a_norm2, a_ffn_in, a_ffn_conv, a_ffn_conv_b, a_ffn_out, kv_norm, w_kv, b_f, b_norm1, b_w_q, b_mem_norm, b_w_mem_kv, b_w_out, b_norm2, b_ffn_in, b_ffn_conv, b_ffn_conv_b, b_ffn_out, final_norm):
    n_a, n_b = a_norm1.shape[0], b_norm1.shape[0]
    assert n_a == 1 and n_b == 1, "one gMLP layer, then one forgetting-attention layer"
    batch, seq, _ = x.shape
    tm = ROW_TM
    row = lambda a: a.reshape(1, -1)
    bf = lambda a: a.astype(BF16)

    kvbd = _memkv(mem,
                  jnp.concatenate([a_mem_norm, b_mem_norm], axis=0)[:, None, :],
                  bf(jnp.concatenate([a_w_mem_kv, b_w_mem_kv], axis=0)))

    b_full = jnp.repeat(a_b_s[0].T, TOK_WIDTH // GMLP_GROUPS, axis=1)
    x = _mixer_a(x, row(a_norm1[0]), bf(a_w_in[0]), row(a_v_norm[0]), a_w_s[0], b_full,
                 kvbd, 0, bf(a_w_out[0]), tm)
    x = _ffn(x, row(a_norm2[0]), bf(a_ffn_in[0]), a_ffn_conv[0], row(a_ffn_conv_b[0]),
             bf(a_ffn_out[0]), row(final_norm), FFN_TM, False)

    w_f = jnp.pad(w_kv[:, 2 * TOK_WIDTH:], ((0, 0), (0, LANES - N_FOX_HEADS)))
    b_f_pad = jnp.pad(b_f, (0, LANES - N_FOX_HEADS)).reshape(1, LANES)

    for j in range(n_b):
        k_sh, v_sh, c_tok, c_head, q_fox, q_mem = _proj_b(
            x, row(kv_norm), bf(w_kv[:, :2 * TOK_WIDTH]), bf(w_f), b_f_pad,
            row(b_norm1[j]), bf(b_w_q[j]), tm)
        c_rows = c_head[:, :N_FOX_HEADS, :].reshape(batch, N_FOX_HEADS // 2, 2, seq)
        tok = _fox_attention(q_fox, k_sh, v_sh, c_tok, c_rows, FOX_TQ)
        x = _mixer_b_out(x, tok, q_mem, kvbd, n_a + j, bf(b_w_out[j]), tm)
        x = _ffn(x, row(b_norm2[j]), bf(b_ffn_in[j]), b_ffn_conv[j], row(b_ffn_conv_b[j]),
                 bf(b_ffn_out[j]), row(final_norm), FFN_TM, j == n_b - 1)
    return x
```

```python
import functools

import jax
import jax.numpy as jnp
from jax import lax
from jax.experimental import pallas as pl
from jax.experimental.pallas import tpu as pltpu

D_MODEL = 1024
N_MEM = 256
HEAD_DIM = 64
N_MEM_HEADS = 4
MEM_WIDTH = N_MEM_HEADS * HEAD_DIM
TOK_WIDTH = D_MODEL - MEM_WIDTH
N_FOX_HEADS = TOK_WIDTH // HEAD_DIM
GMLP_BLOCK = 128
GMLP_GROUPS = 4
D_FF = 2816
CONV_WIDTH = 3
EPS = 1e-6
Q_SCALE = HEAD_DIM ** -0.5

LANES = 128
SUBLANES = 8
HEAD_PAIR = 2 * HEAD_DIM
FF_CHUNK = 256
FOX_TQ = 512
FOX_TK = 512
FFN_TM = 512
PROJ_TM = 512
MIX_A_TM = 1024
IN_CHUNK = 256
MIX_B_TM = 1024
MEM_ROWS = 256
VMEM_LIMIT = 56 * 1024 * 1024

BF16 = jnp.bfloat16
F32 = jnp.float32


def _const_spec(shape):
    nd = len(shape)
    return pl.BlockSpec(shape, lambda *_: (0,) * nd, pipeline_mode=pl.Buffered(1))


def _params(n_axes):
    return pltpu.CompilerParams(
        dimension_semantics=("arbitrary",) * n_axes, vmem_limit_bytes=VMEM_LIMIT)


def _rms(x, g):
    r = lax.rsqrt(jnp.mean(x * x, axis=-1, keepdims=True) + EPS)
    return x * r * g


def _dot(a, b):
    return jnp.dot(a, b, preferred_element_type=F32)


def _dot_nt(a, b):
    return lax.dot_general(a, b, (((1,), (1,)), ((), ())), preferred_element_type=F32)


def _memkv_kernel(mem_ref, g_ref, w_ref, kbd_ref, vbd_ref):
    mn = _rms(mem_ref[0], g_ref[0]).astype(BF16)
    kv = _dot(mn, w_ref[0])
    k, v = kv[:, :MEM_WIDTH], kv[:, MEM_WIDTH:]
    feat_head = lax.broadcasted_iota(jnp.int32, (N_MEM, MEM_WIDTH), 1) // HEAD_DIM
    for h in range(N_MEM_HEADS):
        rows = slice(h * N_MEM, (h + 1) * N_MEM)
        kbd_ref[0, 0, rows, :] = jnp.where(feat_head == h, k, 0.0).astype(BF16)
        vbd_ref[0, 0, rows, :] = jnp.where(feat_head == h, v, 0.0).astype(BF16)


def _memkv(mem, norms, ws):
    n_layers, batch = norms.shape[0], mem.shape[0]
    out = jax.ShapeDtypeStruct((n_layers, batch, N_MEM_HEADS * N_MEM, MEM_WIDTH), BF16)
    out_spec = pl.BlockSpec((1, 1, N_MEM_HEADS * N_MEM, MEM_WIDTH), lambda l, b: (l, b, 0, 0))
    return pl.pallas_call(
        _memkv_kernel,
        grid=(n_layers, batch),
        in_specs=[
            pl.BlockSpec((1, N_MEM, D_MODEL), lambda l, b: (b, 0, 0)),
            pl.BlockSpec((1, 1, D_MODEL), lambda l, b: (l, 0, 0)),
            pl.BlockSpec((1, D_MODEL, 2 * MEM_WIDTH), lambda l, b: (l, 0, 0)),
        ],
        out_specs=[out_spec, out_spec],
        out_shape=[out, out],
        compiler_params=_params(2),
        name="memkv",
    )(mem, norms, ws)


def _mem_softmax(logits):
    probs = []
    for h in range(N_MEM_HEADS):
        seg = logits[:, h * N_MEM:(h + 1) * N_MEM]
        e = jnp.exp(seg - jnp.max(seg, axis=-1, keepdims=True))
        probs.append((e / jnp.sum(e, axis=-1, keepdims=True)).astype(BF16))
    return jnp.concatenate(probs, axis=-1)


def _mem_attention(qs, kbd_ref, vbd_ref):
    logits = [_dot_nt(q, kbd_ref[0, 0]) for q in qs]
    probs = [_mem_softmax(l) for l in logits]
    return [_dot(p, vbd_ref[0, 0]) for p in probs]


def _gmlp_mix(vn, ws_ref):
    tm = vn.shape[0]
    tri = (lax.broadcasted_iota(jnp.int32, (GMLP_BLOCK, GMLP_BLOCK), 0)
           >= lax.broadcasted_iota(jnp.int32, (GMLP_BLOCK, GMLP_BLOCK), 1))
    w = [jnp.where(tri, ws_ref[g], 0.0).astype(BF16) for g in range(GMLP_GROUPS)]
    first_half = lax.broadcasted_iota(jnp.int32, (GMLP_BLOCK, LANES), 1) < HEAD_DIM
    win_lo = (0, 128, 384, 512)
    blocks = []
    for r in range(tm // GMLP_BLOCK):
        rows = slice(r * GMLP_BLOCK, (r + 1) * GMLP_BLOCK)
        m = [_dot(w[g], vn[rows, win_lo[g]:win_lo[g] + 2 * LANES]) for g in range(GMLP_GROUPS)]
        blocks.append(jnp.concatenate([
            m[0][:, :LANES],
            jnp.where(first_half, m[0][:, LANES:], m[1][:, :LANES]),
            m[1][:, LANES:],
            m[2][:, :LANES],
            jnp.where(first_half, m[2][:, LANES:], m[3][:, :LANES]),
            m[3][:, LANES:],
        ], axis=-1))
    return jnp.concatenate(blocks, axis=0)


def _mixer_a_kernel(x_ref, g1_ref, win_ref, vg_ref, ws_ref, bs_ref, kbd_ref, vbd_ref, wout_ref,
                    o_ref):
    x = x_ref[0]
    tm = x.shape[0]
    h = _rms(x, g1_ref[...]).astype(BF16)
    n_tok = TOK_WIDTH // IN_CHUNK

    def z_chunk(c):
        return _dot(h, win_ref[:, c * IN_CHUNK:(c + 1) * IN_CHUNK])

    zv = [z_chunk(n_tok)]
    v = []
    for c in range(1, n_tok):
        zv.append(z_chunk(n_tok + c))
        v.append(jax.nn.gelu(zv[c - 1]))
    zq = z_chunk(2 * n_tok)
    v.append(jax.nn.gelu(zv[n_tok - 1]))
    zu = [z_chunk(0)]
    ssq = sum(jnp.sum(vc * vc, axis=-1, keepdims=True) for vc in v)
    r = lax.rsqrt(ssq * (1.0 / TOK_WIDTH) + EPS)
    vn = jnp.concatenate(
        [(vc * r * vg_ref[:, c * IN_CHUNK:(c + 1) * IN_CHUNK]).astype(BF16)
         for c, vc in enumerate(v)], axis=-1)
    q_mem = (zq * Q_SCALE).astype(BF16)
    blocks = [slice(r0, r0 + MEM_ROWS) for r0 in range(0, tm, MEM_ROWS)]
    logits = [_dot_nt(q_mem[rows], kbd_ref[0, 0]) for rows in blocks]
    u, probs = [], []
    for c in range(1, n_tok):
        zu.append(z_chunk(c))
        u.append(jax.nn.gelu(zu[c - 1]))
        if c - 1 < len(logits):
            probs.append(_mem_softmax(logits[c - 1]))
    probs.extend(_mem_softmax(l) for l in logits[len(probs):])
    mem_o = [_dot(p, vbd_ref[0, 0]) for p in probs]
    u.append(jax.nn.gelu(zu[n_tok - 1]))
    bias = jnp.concatenate([bs_ref[...]] * (tm // GMLP_BLOCK), axis=0)
    tok = (jnp.concatenate(u, axis=-1) * (_gmlp_mix(vn, ws_ref) + bias)).astype(BF16)
    cat = jnp.concatenate([tok, jnp.concatenate(mem_o, axis=0).astype(BF16)], axis=-1)
    o_ref[0] = x + _dot(cat, wout_ref[...])


def _mixer_a(x, g1, w_in, v_g, w_s, b_full, kvbd, layer, w_out, tm):
    batch, seq, _ = x.shape
    xspec = pl.BlockSpec((1, tm, D_MODEL), lambda b, s: (b, s, 0))
    kvspec = pl.BlockSpec((1, 1, N_MEM_HEADS * N_MEM, MEM_WIDTH), lambda b, s: (layer, b, 0, 0))
    return pl.pallas_call(
        _mixer_a_kernel,
        grid=(batch, seq // tm),
        in_specs=[
            xspec,
            _const_spec((1, D_MODEL)),
            _const_spec(w_in.shape),
            _const_spec((1, TOK_WIDTH)),
            _const_spec(w_s.shape),
            _const_spec(b_full.shape),
            kvspec, kvspec,
            _const_spec(w_out.shape),
        ],
        out_specs=xspec,
        out_shape=jax.ShapeDtypeStruct(x.shape, F32),
        compiler_params=_params(2),
        name="mixer_a",
    )(x, g1, w_in, v_g, w_s, b_full, kvbd[0], kvbd[1], w_out)


def _shift_rows(h, prev, n):
    rolled = pltpu.roll(h, n, 0)
    head = jnp.where(lax.broadcasted_iota(jnp.int32, prev.shape, 0) < n,
                     pltpu.roll(prev, n, 0), rolled[:SUBLANES])
    return jnp.concatenate([head, rolled[SUBLANES:]], axis=0)


def _ffn_kernel(x_ref, g_ref, win_ref, cw_ref, cb_ref, wout_ref, fn_ref, o_ref, carry_ref, act_ref,
                *, final_norm):
    @pl.when(pl.program_id(1) == 0)
    def _():
        carry_ref[...] = jnp.zeros_like(carry_ref)

    x = x_ref[0]
    tm = x.shape[0]
    hn = _rms(x, g_ref[...]).astype(BF16)

    def up_proj(col0):
        return _dot(hn, win_ref[:, col0:col0 + FF_CHUNK])

    def conv_cols(h, col0):
        cols = slice(col0, col0 + FF_CHUNK)
        prev = carry_ref[:, cols]
        carry_ref[:, cols] = h[tm - SUBLANES:]
        w = cw_ref[:, cols]
        return (cb_ref[:, cols] + w[2:3] * h
                + w[0:1] * _shift_rows(h, prev, 2) + w[1:2] * _shift_rows(h, prev, 1))

    for c in range(D_FF // FF_CHUNK):
        gate = conv_cols(up_proj(c * FF_CHUNK), c * FF_CHUNK)
        up = conv_cols(up_proj(D_FF + c * FF_CHUNK), D_FF + c * FF_CHUNK)
        act_ref[:, c * FF_CHUNK:(c + 1) * FF_CHUNK] = (
            gate * (1.0 / (1.0 + jnp.exp(-gate))) * up).astype(BF16)
    y = x + _dot(act_ref[...], wout_ref[...])
    if final_norm:
        y = _rms(y, fn_ref[...])
    o_ref[0] = y


def _ffn(x, g, w_in, conv_w, conv_b, w_out, fn, tm, final_norm):
    batch, seq, _ = x.shape
    xspec = pl.BlockSpec((1, tm, D_MODEL), lambda b, s: (b, s, 0))
    return pl.pallas_call(
        functools.partial(_ffn_kernel, final_norm=final_norm),
        grid=(batch, seq // tm),
        in_specs=[
            xspec,
            _const_spec((1, D_MODEL)),
            _const_spec(w_in.shape),
            _const_spec(conv_w.shape),
            _const_spec(conv_b.shape),
            _const_spec(w_out.shape),
            _const_spec((1, D_MODEL)),
        ],
        out_specs=xspec,
        out_shape=jax.ShapeDtypeStruct(x.shape, F32),
        scratch_shapes=[pltpu.VMEM((SUBLANES, 2 * D_FF), F32), pltpu.VMEM((tm, D_FF), BF16)],
        compiler_params=_params(2),
        name="ffn_final" if final_norm else "ffn",
    )(x, g, w_in, conv_w, conv_b, w_out, fn)


def _split3(x):
    hi = x.astype(BF16)
    r = x - hi.astype(F32)
    mid = r.astype(BF16)
    lo = (r - mid.astype(F32)).astype(BF16)
    return hi, mid, lo


def _proj_b_kernel(x_ref, gkv_ref, wkv_ref, wf_ref, bf_ref, gq_ref, wq_ref,
                   k_ref, v_ref, ct_ref, cr_ref, qf_ref, qm_ref, carry_ref):
    @pl.when(pl.program_id(1) == 0)
    def _():
        carry_ref[...] = jnp.zeros_like(carry_ref)

    x = x_ref[0]
    tm = x.shape[0]
    xr = x * lax.rsqrt(jnp.mean(x * x, axis=-1, keepdims=True) + EPS)
    hkv = (xr * gkv_ref[...]).astype(BF16)
    kv = _dot(hkv, wkv_ref[...])
    k_ref[0] = kv[:, :TOK_WIDTH].astype(BF16)
    v_ref[0] = kv[:, TOK_WIDTH:].astype(BF16)

    f = _dot(hkv, wf_ref[...]) + bf_ref[...]
    log_f = jnp.minimum(f, 0.0) - jnp.log1p(jnp.exp(-jnp.abs(f)))
    tri = (lax.broadcasted_iota(jnp.int32, (tm, tm), 0)
           >= lax.broadcasted_iota(jnp.int32, (tm, tm), 1)).astype(BF16)
    hi, mid, lo = _split3(log_f)
    c = carry_ref[0:1, :] + ((_dot(tri, hi) + _dot(tri, mid)) + _dot(tri, lo))
    carry_ref[...] = jnp.broadcast_to(c[tm - 1:tm, :], carry_ref.shape)
    ct_ref[0] = c
    cr_ref[0] = c.T[:2 * SUBLANES, :]

    z = _dot((xr * gq_ref[...]).astype(BF16), wq_ref[...]) * Q_SCALE
    qf_ref[0] = z[:, :TOK_WIDTH].astype(BF16)
    qm_ref[0] = z[:, TOK_WIDTH:].astype(BF16)


def _proj_b(x, g_kv, w_kv, w_f, b_f, g_q, w_q, tm):
    batch, seq, _ = x.shape

    def tile(width):
        return pl.BlockSpec((1, tm, width), lambda b, s: (b, s, 0))

    return pl.pallas_call(
        _proj_b_kernel,
        grid=(batch, seq // tm),
        in_specs=[
            tile(D_MODEL),
            _const_spec((1, D_MODEL)),
            _const_spec(w_kv.shape),
            _const_spec(w_f.shape),
            _const_spec((1, LANES)),
            _const_spec((1, D_MODEL)),
            _const_spec(w_q.shape),
        ],
        out_specs=[
            tile(TOK_WIDTH), tile(TOK_WIDTH), tile(LANES),
            pl.BlockSpec((1, 2 * SUBLANES, tm), lambda b, s: (b, 0, s)),
            tile(TOK_WIDTH), tile(MEM_WIDTH),
        ],
        out_shape=[
            jax.ShapeDtypeStruct((batch, seq, TOK_WIDTH), BF16),
            jax.ShapeDtypeStruct((batch, seq, TOK_WIDTH), BF16),
            jax.ShapeDtypeStruct((batch, seq, LANES), F32),
            jax.ShapeDtypeStruct((batch, 2 * SUBLANES, seq), F32),
            jax.ShapeDtypeStruct((batch, seq, TOK_WIDTH), BF16),
            jax.ShapeDtypeStruct((batch, seq, MEM_WIDTH), BF16),
        ],
        scratch_shapes=[pltpu.VMEM((SUBLANES, LANES), F32)],
        compiler_params=_params(2),
        name="proj_b",
    )(x, g_kv, w_kv, w_f, b_f, g_q, w_q)


def _fox_tile(n_keys, pair, q_ref, k_ref, v_ref, ct_ref, cr_ref, o_ref):
    tq, tk = q_ref.shape[1], FOX_TK
    n_tiles = n_keys // tk
    q2 = q_ref[0]
    ct_all = ct_ref[0]
    lane = lax.broadcasted_iota(jnp.int32, (tq, LANES), 1)
    low_half = lane < HEAD_DIM
    v_low = lax.broadcasted_iota(jnp.int32, (tk, LANES), 1) < HEAD_DIM
    q_pos = lax.broadcasted_iota(jnp.int32, (tq, tk), 0) + (n_keys - tq)
    k_pos = lax.broadcasted_iota(jnp.int32, (tq, tk), 1)
    heads = (0, 1)
    qh = [jnp.where(low_half if hh == 0 else ~low_half, q2, jnp.zeros_like(q2)) for hh in heads]
    ct = [jnp.sum(jnp.where(lane == 2 * pair + hh, ct_all, 0.0), axis=-1, keepdims=True)
          for hh in heads]

    s = [[None] * n_tiles for _ in heads]
    m_rows = [None, None]
    for t in range(n_tiles):
        keys = slice(t * tk, (t + 1) * tk)
        k_t = k_ref[0, keys, :]
        for hh in heads:
            s_t = _dot_nt(qh[hh], k_t) - cr_ref[0, 0, hh:hh + 1, keys]
            if (t + 1) * tk > n_keys - tq:
                s_t = jnp.where(q_pos >= k_pos + t * tk, s_t, -jnp.inf)
            s[hh][t] = s_t
            m_t = jnp.max(s_t, axis=-1, keepdims=True)
            m_rows[hh] = m_t if t == 0 else jnp.maximum(m_rows[hh], m_t)

    shift = []
    for hh in heads:
        shift.append(ct[hh] - (m_rows[hh] + ct[hh]))
    pv = [None, None]
    for t in range(n_tiles):
        v_t = v_ref[0, t * tk:(t + 1) * tk, :]
        for hh in heads:
            p_t = jnp.exp(s[hh][t] + shift[hh]).astype(BF16)
            v_h = jnp.where(v_low if hh == 0 else ~v_low, v_t, jnp.ones_like(v_t))
            pv_t = _dot(p_t, v_h)
            pv[hh] = pv_t if t == 0 else pv[hh] + pv_t
    outs = [pv[hh] / pltpu.roll(pv[hh], HEAD_DIM, 1) for hh in heads]
    o_ref[0] = jnp.where(low_half, outs[0], outs[1]).astype(BF16)


def _fox_kernel(q_ref, k_ref, v_ref, ct_ref, cr_ref, o_ref):
    tq, seq = q_ref.shape[1], k_ref.shape[1]
    pair = pl.program_id(1)
    qi = pl.program_id(2)
    for i in range(seq // tq):
        pl.when(qi == i)(functools.partial(
            _fox_tile, (i + 1) * tq, pair, q_ref, k_ref, v_ref, ct_ref, cr_ref, o_ref))


def _fox_attention(q, k, v, c_tok, c_rows, tq):
    batch, seq, _ = q.shape
    n_pairs = TOK_WIDTH // HEAD_PAIR
    qspec = pl.BlockSpec((1, tq, HEAD_PAIR), lambda b, p, i: (b, i, p))
    kvspec = pl.BlockSpec((1, seq, HEAD_PAIR), lambda b, p, i: (b, 0, p))
    return pl.pallas_call(
        _fox_kernel,
        grid=(batch, n_pairs, seq // tq),
        in_specs=[
            qspec, kvspec, kvspec,
            pl.BlockSpec((1, tq, LANES), lambda b, p, i: (b, i, 0)),
            pl.BlockSpec((1, 1, 2, seq), lambda b, p, i: (b, p, 0, 0)),
        ],
        out_specs=qspec,
        out_shape=jax.ShapeDtypeStruct((batch, seq, TOK_WIDTH), BF16),
        compiler_params=_params(3),
        name="fox_attention",
    )(q, k, v, c_tok, c_rows)


def _mixer_b_out_kernel(x_ref, tok_ref, qm_ref, kbd_ref, vbd_ref, wout_ref, o_ref):
    tm = x_ref.shape[1]
    blocks = [slice(r, r + MEM_ROWS) for r in range(0, tm, MEM_ROWS)]
    mem_o = _mem_attention([qm_ref[0, rows, :] for rows in blocks], kbd_ref, vbd_ref)
    mem_o = jnp.concatenate(mem_o, axis=0).astype(BF16)
    cat = jnp.concatenate([tok_ref[0], mem_o], axis=-1)
    o_ref[0] = x_ref[0] + _dot(cat, wout_ref[...])


def _mixer_b_out(x, tok, q_mem, kvbd, layer, w_out, tm):
    batch, seq, _ = x.shape

    def tile(width):
        return pl.BlockSpec((1, tm, width), lambda b, s: (b, s, 0))

    kvspec = pl.BlockSpec((1, 1, N_MEM_HEADS * N_MEM, MEM_WIDTH), lambda b, s: (layer, b, 0, 0))
    return pl.pallas_call(
        _mixer_b_out_kernel,
        grid=(batch, seq // tm),
        in_specs=[tile(D_MODEL), tile(TOK_WIDTH), tile(MEM_WIDTH), kvspec, kvspec,
                  _const_spec(w_out.shape)],
        out_specs=tile(D_MODEL),
        out_shape=jax.ShapeDtypeStruct(x.shape, F32),
        compiler_params=_params(2),
        name="mixer_b_out",
    )(x, tok, q_mem, kvbd[0], kvbd[1], w_out)


def kernel(x, mem, a_norm1, a_w_in, a_v_norm, a_w_s, a_b_s, a_mem_norm, a_w_mem_kv, a_w_out, a_norm2, a_ffn_in, a_ffn_conv, a_ffn_conv_b, a_ffn_out, kv_norm, w_kv, b_f, b_norm1, b_w_q, b_mem_norm, b_w_mem_kv, b_w_out, b_norm2, b_ffn_in, b_ffn_conv, b_ffn_conv_b, b_ffn_out, final_norm):
    n_a, n_b = a_norm1.shape[0], b_norm1.shape[0]
    assert n_a == 1 and n_b == 1, "one gMLP layer, then one forgetting-attention layer"
    batch, seq, _ = x.shape
    row = lambda a: a.reshape(1, -1)
    bf = lambda a: a.astype(BF16)

    kvbd = _memkv(mem,
                  jnp.concatenate([a_mem_norm, b_mem_norm], axis=0)[:, None, :],
                  bf(jnp.concatenate([a_w_mem_kv, b_w_mem_kv], axis=0)))

    b_full = jnp.repeat(a_b_s[0].T, TOK_WIDTH // GMLP_GROUPS, axis=1)
    x = _mixer_a(x, row(a_norm1[0]), bf(a_w_in[0]), row(a_v_norm[0]), a_w_s[0], b_full,
                 kvbd, 0, bf(a_w_out[0]), MIX_A_TM)
    x = _ffn(x, row(a_norm2[0]), bf(a_ffn_in[0]), a_ffn_conv[0], row(a_ffn_conv_b[0]),
             bf(a_ffn_out[0]), row(final_norm), FFN_TM, False)

    w_f = jnp.pad(w_kv[:, 2 * TOK_WIDTH:], ((0, 0), (0, LANES - N_FOX_HEADS)))
    b_f_pad = jnp.pad(b_f, (0, LANES - N_FOX_HEADS)).reshape(1, LANES)

    for j in range(n_b):
        k_sh, v_sh, c_tok, c_head, q_fox, q_mem = _proj_b(
            x, row(kv_norm), bf(w_kv[:, :2 * TOK_WIDTH]), bf(w_f), b_f_pad,
            row(b_norm1[j]), bf(b_w_q[j]), PROJ_TM)
        c_rows = c_head[:, :N_FOX_HEADS, :].reshape(batch, N_FOX_HEADS // 2, 2, seq)
        tok = _fox_attention(q_fox, k_sh, v_sh, c_tok, c_rows, FOX_TQ)
        x = _mixer_b_out(x, tok, q_mem, kvbd, n_a + j, bf(b_w_out[j]), MIX_B_TM)
        x = _ffn(x, row(b_norm2[j]), bf(b_ffn_in[j]), b_ffn_conv[j], row(b_ffn_conv_b[j]),
                 bf(b_ffn_out[j]), row(final_norm), FFN_TM, j == n_b - 1)
    return x
```

```python
import functools

import jax
import jax.numpy as jnp
from jax import lax
from jax.experimental import pallas as pl
from jax.experimental.pallas import tpu as pltpu

D_MODEL = 1024
N_MEM = 256
HEAD_DIM = 64
N_MEM_HEADS = 4
MEM_WIDTH = N_MEM_HEADS * HEAD_DIM
TOK_WIDTH = D_MODEL - MEM_WIDTH
N_FOX_HEADS = TOK_WIDTH // HEAD_DIM
GMLP_BLOCK = 128
GMLP_GROUPS = 4
D_FF = 2816
CONV_WIDTH = 3
EPS = 1e-6
Q_SCALE = HEAD_DIM ** -0.5

LANES = 128
SUBLANES = 8
HEAD_PAIR = 2 * HEAD_DIM
FF_CHUNK = 256
FOX_TQ = 512
FOX_PAIRS = 3
FOX_TK = 512
FFN_TM = 512
PROJ_TM = 512
MIX_A_TM = 1024
IN_CHUNK = 256
MIX_B_TM = 1024
MEM_ROWS = 256
VMEM_LIMIT = 56 * 1024 * 1024

BF16 = jnp.bfloat16
F32 = jnp.float32


def _const_spec(shape):
    nd = len(shape)
    return pl.BlockSpec(shape, lambda *_: (0,) * nd, pipeline_mode=pl.Buffered(1))


def _params(n_axes):
    return pltpu.CompilerParams(
        dimension_semantics=("arbitrary",) * n_axes, vmem_limit_bytes=VMEM_LIMIT)


def _rms(x, g):
    r = lax.rsqrt(jnp.mean(x * x, axis=-1, keepdims=True) + EPS)
    return x * r * g


def _dot(a, b):
    return jnp.dot(a, b, preferred_element_type=F32)


def _dot_nt(a, b):
    return lax.dot_general(a, b, (((1,), (1,)), ((), ())), preferred_element_type=F32)


def _memkv_kernel(mem_ref, g_ref, w_ref, kbd_ref, vbd_ref):
    mn = _rms(mem_ref[0], g_ref[0]).astype(BF16)
    kv = _dot(mn, w_ref[0])
    k, v = kv[:, :MEM_WIDTH], kv[:, MEM_WIDTH:]
    feat_head = lax.broadcasted_iota(jnp.int32, (N_MEM, MEM_WIDTH), 1) // HEAD_DIM
    for h in range(N_MEM_HEADS):
        rows = slice(h * N_MEM, (h + 1) * N_MEM)
        kbd_ref[0, 0, rows, :] = jnp.where(feat_head == h, k, 0.0).astype(BF16)
        vbd_ref[0, 0, rows, :] = jnp.where(feat_head == h, v, 0.0).astype(BF16)


def _memkv(mem, norms, ws):
    n_layers, batch = norms.shape[0], mem.shape[0]
    out = jax.ShapeDtypeStruct((n_layers, batch, N_MEM_HEADS * N_MEM, MEM_WIDTH), BF16)
    out_spec = pl.BlockSpec((1, 1, N_MEM_HEADS * N_MEM, MEM_WIDTH), lambda l, b: (l, b, 0, 0))
    return pl.pallas_call(
        _memkv_kernel,
        grid=(n_layers, batch),
        in_specs=[
            pl.BlockSpec((1, N_MEM, D_MODEL), lambda l, b: (b, 0, 0)),
            pl.BlockSpec((1, 1, D_MODEL), lambda l, b: (l, 0, 0)),
            pl.BlockSpec((1, D_MODEL, 2 * MEM_WIDTH), lambda l, b: (l, 0, 0)),
        ],
        out_specs=[out_spec, out_spec],
        out_shape=[out, out],
        compiler_params=_params(2),
        name="memkv",
    )(mem, norms, ws)


def _mem_softmax(logits):
    probs = []
    for h in range(N_MEM_HEADS):
        seg = logits[:, h * N_MEM:(h + 1) * N_MEM]
        e = jnp.exp(seg - jnp.max(seg, axis=-1, keepdims=True))
        probs.append((e / jnp.sum(e, axis=-1, keepdims=True)).astype(BF16))
    return jnp.concatenate(probs, axis=-1)


def _mem_attention(qs, kbd_ref, vbd_ref):
    logits = [_dot_nt(q, kbd_ref[0, 0]) for q in qs]
    probs = [_mem_softmax(l) for l in logits]
    return [_dot(p, vbd_ref[0, 0]) for p in probs]


def _gmlp_mix(vn, ws_ref):
    tm = vn.shape[0]
    tri = (lax.broadcasted_iota(jnp.int32, (GMLP_BLOCK, GMLP_BLOCK), 0)
           >= lax.broadcasted_iota(jnp.int32, (GMLP_BLOCK, GMLP_BLOCK), 1))
    w = [jnp.where(tri, ws_ref[g], 0.0).astype(BF16) for g in range(GMLP_GROUPS)]
    first_half = lax.broadcasted_iota(jnp.int32, (GMLP_BLOCK, LANES), 1) < HEAD_DIM
    win_lo = (0, 128, 384, 512)
    blocks = []
    for r in range(tm // GMLP_BLOCK):
        rows = slice(r * GMLP_BLOCK, (r + 1) * GMLP_BLOCK)
        m = [_dot(w[g], vn[rows, win_lo[g]:win_lo[g] + 2 * LANES]) for g in range(GMLP_GROUPS)]
        blocks.append(jnp.concatenate([
            m[0][:, :LANES],
            jnp.where(first_half, m[0][:, LANES:], m[1][:, :LANES]),
            m[1][:, LANES:],
            m[2][:, :LANES],
            jnp.where(first_half, m[2][:, LANES:], m[3][:, :LANES]),
            m[3][:, LANES:],
        ], axis=-1))
    return jnp.concatenate(blocks, axis=0)


def _mixer_a_kernel(x_ref, g1_ref, win_ref, vg_ref, ws_ref, bs_ref, kbd_ref, vbd_ref, wout_ref,
                    o_ref):
    x = x_ref[0]
    tm = x.shape[0]
    h = _rms(x, g1_ref[...]).astype(BF16)
    n_tok = TOK_WIDTH // IN_CHUNK

    def z_chunk(c):
        return _dot(h, win_ref[:, c * IN_CHUNK:(c + 1) * IN_CHUNK])

    zv = [z_chunk(n_tok)]
    v = []
    for c in range(1, n_tok):
        zv.append(z_chunk(n_tok + c))
        v.append(jax.nn.gelu(zv[c - 1]))
    zq = z_chunk(2 * n_tok)
    v.append(jax.nn.gelu(zv[n_tok - 1]))
    zu = [z_chunk(0)]
    ssq = sum(jnp.sum(vc * vc, axis=-1, keepdims=True) for vc in v)
    r = lax.rsqrt(ssq * (1.0 / TOK_WIDTH) + EPS)
    vn = jnp.concatenate(
        [(vc * r * vg_ref[:, c * IN_CHUNK:(c + 1) * IN_CHUNK]).astype(BF16)
         for c, vc in enumerate(v)], axis=-1)
    q_mem = (zq * Q_SCALE).astype(BF16)
    blocks = [slice(r0, r0 + MEM_ROWS) for r0 in range(0, tm, MEM_ROWS)]
    logits = [_dot_nt(q_mem[rows], kbd_ref[0, 0]) for rows in blocks]
    u, probs = [], []
    for c in range(1, n_tok):
        zu.append(z_chunk(c))
        u.append(jax.nn.gelu(zu[c - 1]))
        if c - 1 < len(logits):
            probs.append(_mem_softmax(logits[c - 1]))
    probs.extend(_mem_softmax(l) for l in logits[len(probs):])
    mem_o = [_dot(p, vbd_ref[0, 0]) for p in probs]
    u.append(jax.nn.gelu(zu[n_tok - 1]))
    bias = jnp.concatenate([bs_ref[...]] * (tm // GMLP_BLOCK), axis=0)
    tok = (jnp.concatenate(u, axis=-1) * (_gmlp_mix(vn, ws_ref) + bias)).astype(BF16)
    cat = jnp.concatenate([tok, jnp.concatenate(mem_o, axis=0).astype(BF16)], axis=-1)
    o_ref[0] = x + _dot(cat, wout_ref[...])


def _mixer_a(x, g1, w_in, v_g, w_s, b_full, kvbd, layer, w_out, tm):
    batch, seq, _ = x.shape
    xspec = pl.BlockSpec((1, tm, D_MODEL), lambda b, s: (b, s, 0))
    kvspec = pl.BlockSpec((1, 1, N_MEM_HEADS * N_MEM, MEM_WIDTH), lambda b, s: (layer, b, 0, 0))
    return pl.pallas_call(
        _mixer_a_kernel,
        grid=(batch, seq // tm),
        in_specs=[
            xspec,
            _const_spec((1, D_MODEL)),
            _const_spec(w_in.shape),
            _const_spec((1, TOK_WIDTH)),
            _const_spec(w_s.shape),
            _const_spec(b_full.shape),
            kvspec, kvspec,
            _const_spec(w_out.shape),
        ],
        out_specs=xspec,
        out_shape=jax.ShapeDtypeStruct(x.shape, F32),
        compiler_params=_params(2),
        name="mixer_a",
    )(x, g1, w_in, v_g, w_s, b_full, kvbd[0], kvbd[1], w_out)


def _shift_rows(h, prev, n):
    rolled = pltpu.roll(h, n, 0)
    head = jnp.where(lax.broadcasted_iota(jnp.int32, prev.shape, 0) < n,
                     pltpu.roll(prev, n, 0), rolled[:SUBLANES])
    return jnp.concatenate([head, rolled[SUBLANES:]], axis=0)


def _ffn_kernel(x_ref, g_ref, win_ref, cw_ref, cb_ref, wout_ref, fn_ref, o_ref, carry_ref, act_ref,
                *, final_norm):
    @pl.when(pl.program_id(1) == 0)
    def _():
        carry_ref[...] = jnp.zeros_like(carry_ref)

    x = x_ref[0]
    tm = x.shape[0]
    hn = _rms(x, g_ref[...]).astype(BF16)

    def up_proj(col0):
        return _dot(hn, win_ref[:, col0:col0 + FF_CHUNK])

    def conv_cols(h, col0):
        cols = slice(col0, col0 + FF_CHUNK)
        prev = carry_ref[:, cols]
        carry_ref[:, cols] = h[tm - SUBLANES:]
        w = cw_ref[:, cols]
        return (cb_ref[:, cols] + w[2:3] * h
                + w[0:1] * _shift_rows(h, prev, 2) + w[1:2] * _shift_rows(h, prev, 1))

    for c in range(D_FF // FF_CHUNK):
        gate = conv_cols(up_proj(c * FF_CHUNK), c * FF_CHUNK)
        up = conv_cols(up_proj(D_FF + c * FF_CHUNK), D_FF + c * FF_CHUNK)
        act_ref[:, c * FF_CHUNK:(c + 1) * FF_CHUNK] = (
            gate * (1.0 / (1.0 + jnp.exp(-gate))) * up).astype(BF16)
    y = x + _dot(act_ref[...], wout_ref[...])
    if final_norm:
        y = _rms(y, fn_ref[...])
    o_ref[0] = y


def _ffn(x, g, w_in, conv_w, conv_b, w_out, fn, tm, final_norm):
    batch, seq, _ = x.shape
    xspec = pl.BlockSpec((1, tm, D_MODEL), lambda b, s: (b, s, 0))
    return pl.pallas_call(
        functools.partial(_ffn_kernel, final_norm=final_norm),
        grid=(batch, seq // tm),
        in_specs=[
            xspec,
            _const_spec((1, D_MODEL)),
            _const_spec(w_in.shape),
            _const_spec(conv_w.shape),
            _const_spec(conv_b.shape),
            _const_spec(w_out.shape),
            _const_spec((1, D_MODEL)),
        ],
        out_specs=xspec,
        out_shape=jax.ShapeDtypeStruct(x.shape, F32),
        scratch_shapes=[pltpu.VMEM((SUBLANES, 2 * D_FF), F32), pltpu.VMEM((tm, D_FF), BF16)],
        compiler_params=_params(2),
        name="ffn_final" if final_norm else "ffn",
    )(x, g, w_in, conv_w, conv_b, w_out, fn)


def _split3(x):
    hi = x.astype(BF16)
    r = x - hi.astype(F32)
    mid = r.astype(BF16)
    lo = (r - mid.astype(F32)).astype(BF16)
    return hi, mid, lo


def _proj_b_kernel(x_ref, gkv_ref, wkv_ref, wf_ref, bf_ref, gq_ref, wq_ref,
                   k_ref, v_ref, ct_ref, cr_ref, qf_ref, qm_ref, carry_ref):
    @pl.when(pl.program_id(1) == 0)
    def _():
        carry_ref[...] = jnp.zeros_like(carry_ref)

    x = x_ref[0]
    tm = x.shape[0]
    xr = x * lax.rsqrt(jnp.mean(x * x, axis=-1, keepdims=True) + EPS)
    hkv = (xr * gkv_ref[...]).astype(BF16)
    hq = (xr * gq_ref[...]).astype(BF16)

    def project(h, w_ref, out_ref, c, col0, scale=None):
        z = _dot(h, w_ref[:, col0 + c * IN_CHUNK:col0 + (c + 1) * IN_CHUNK])
        if scale is not None:
            z = z * scale
        out_ref[0, :, c * IN_CHUNK:(c + 1) * IN_CHUNK] = z.astype(BF16)

    n_tok = TOK_WIDTH // IN_CHUNK
    f = _dot(hkv, wf_ref[...]) + bf_ref[...]
    project(hkv, wkv_ref, k_ref, 0, 0)
    log_f = jnp.minimum(f, 0.0) - jnp.log1p(jnp.exp(-jnp.abs(f)))
    tri = (lax.broadcasted_iota(jnp.int32, (tm, tm), 0)
           >= lax.broadcasted_iota(jnp.int32, (tm, tm), 1)).astype(BF16)
    hi, mid, lo = _split3(log_f)
    project(hkv, wkv_ref, k_ref, 1, 0)
    c = carry_ref[0:1, :] + ((_dot(tri, hi) + _dot(tri, mid)) + _dot(tri, lo))
    project(hkv, wkv_ref, k_ref, 2, 0)
    carry_ref[...] = jnp.broadcast_to(c[tm - 1:tm, :], carry_ref.shape)
    ct_ref[0] = c
    cr_ref[0] = c.T[:2 * SUBLANES, :]
    for j in range(n_tok):
        project(hkv, wkv_ref, v_ref, j, TOK_WIDTH)
    for j in range(n_tok):
        project(hq, wq_ref, qf_ref, j, 0, Q_SCALE)
    project(hq, wq_ref, qm_ref, 0, TOK_WIDTH, Q_SCALE)


def _proj_b(x, g_kv, w_kv, w_f, b_f, g_q, w_q, tm):
    batch, seq, _ = x.shape

    def tile(width):
        return pl.BlockSpec((1, tm, width), lambda b, s: (b, s, 0))

    return pl.pallas_call(
        _proj_b_kernel,
        grid=(batch, seq // tm),
        in_specs=[
            tile(D_MODEL),
            _const_spec((1, D_MODEL)),
            _const_spec(w_kv.shape),
            _const_spec(w_f.shape),
            _const_spec((1, LANES)),
            _const_spec((1, D_MODEL)),
            _const_spec(w_q.shape),
        ],
        out_specs=[
            tile(TOK_WIDTH), tile(TOK_WIDTH), tile(LANES),
            pl.BlockSpec((1, 2 * SUBLANES, tm), lambda b, s: (b, 0, s)),
            tile(TOK_WIDTH), tile(MEM_WIDTH),
        ],
        out_shape=[
            jax.ShapeDtypeStruct((batch, seq, TOK_WIDTH), BF16),
            jax.ShapeDtypeStruct((batch, seq, TOK_WIDTH), BF16),
            jax.ShapeDtypeStruct((batch, seq, LANES), F32),
            jax.ShapeDtypeStruct((batch, 2 * SUBLANES, seq), F32),
            jax.ShapeDtypeStruct((batch, seq, TOK_WIDTH), BF16),
            jax.ShapeDtypeStruct((batch, seq, MEM_WIDTH), BF16),
        ],
        scratch_shapes=[pltpu.VMEM((SUBLANES, LANES), F32)],
        compiler_params=_params(2),
        name="proj_b",
    )(x, g_kv, w_kv, w_f, b_f, g_q, w_q)


def _fox_tile(n_keys, group, q_ref, k_ref, v_ref, ct_ref, cr_ref, o_ref):
    tq, tk = q_ref.shape[1], FOX_TK
    n_tiles = n_keys // tk
    ct_all = ct_ref[0]
    lane = lax.broadcasted_iota(jnp.int32, (tq, LANES), 1)
    low_half = lane < HEAD_DIM
    v_low = lax.broadcasted_iota(jnp.int32, (tk, LANES), 1) < HEAD_DIM
    q_pos = lax.broadcasted_iota(jnp.int32, (tq, tk), 0) + (n_keys - tq)
    k_pos = lax.broadcasted_iota(jnp.int32, (tq, tk), 1)
    heads = range(2 * FOX_PAIRS)

    def pair_lanes(hh):
        return slice((hh // 2) * HEAD_PAIR, (hh // 2 + 1) * HEAD_PAIR)

    qh, ct = [], []
    for hh in heads:
        q2 = q_ref[0, :, pair_lanes(hh)]
        qh.append(jnp.where(low_half if hh % 2 == 0 else ~low_half, q2, jnp.zeros_like(q2)))
        head = 2 * FOX_PAIRS * group + hh
        ct.append(jnp.sum(jnp.where(lane == head, ct_all, 0.0), axis=-1, keepdims=True))

    s = [[None] * n_tiles for _ in heads]
    m_rows = [None for _ in heads]
    shift = [None for _ in heads]
    pv = [None for _ in heads]

    def pass1(hh, t):
        keys = slice(t * tk, (t + 1) * tk)
        k_t = k_ref[0, keys, pair_lanes(hh)]
        s_t = _dot_nt(qh[hh], k_t) - cr_ref[0, hh // 2, hh % 2:hh % 2 + 1, keys]
        if (t + 1) * tk > n_keys - tq:
            s_t = jnp.where(q_pos >= k_pos + t * tk, s_t, -jnp.inf)
        s[hh][t] = s_t
        m_t = jnp.max(s_t, axis=-1, keepdims=True)
        m_rows[hh] = m_t if t == 0 else jnp.maximum(m_rows[hh], m_t)

    def pass2(hh, t):
        if t == 0:
            shift[hh] = ct[hh] - (m_rows[hh] + ct[hh])
        v_t = v_ref[0, t * tk:(t + 1) * tk, pair_lanes(hh)]
        p_t = jnp.exp(s[hh][t] + shift[hh]).astype(BF16)
        v_h = jnp.where(v_low if hh % 2 == 0 else ~v_low, v_t, jnp.ones_like(v_t))
        pv_t = _dot(p_t, v_h)
        pv[hh] = pv_t if t == 0 else pv[hh] + pv_t

    for pp in range(FOX_PAIRS + 1):
        for t in range(n_tiles):
            for hh in (2 * pp, 2 * pp + 1):
                if pp < FOX_PAIRS:
                    pass1(hh, t)
            for hh in (2 * pp - 2, 2 * pp - 1):
                if pp >= 1:
                    pass2(hh, t)
    outs = [pv[hh] / pltpu.roll(pv[hh], HEAD_DIM, 1) for hh in heads]
    for pp in range(FOX_PAIRS):
        o_ref[0, :, pp * HEAD_PAIR:(pp + 1) * HEAD_PAIR] = jnp.where(
            low_half, outs[2 * pp], outs[2 * pp + 1]).astype(BF16)


def _fox_kernel(q_ref, k_ref, v_ref, ct_ref, cr_ref, o_ref):
    tq, seq = q_ref.shape[1], k_ref.shape[1]
    group = pl.program_id(1)
    qi = pl.program_id(2)
    for i in range(seq // tq):
        pl.when(qi == i)(functools.partial(
            _fox_tile, (i + 1) * tq, group, q_ref, k_ref, v_ref, ct_ref, cr_ref, o_ref))


def _fox_attention(q, k, v, c_tok, c_rows, tq):
    batch, seq, _ = q.shape
    width = FOX_PAIRS * HEAD_PAIR
    qspec = pl.BlockSpec((1, tq, width), lambda b, p, i: (b, i, p))
    kvspec = pl.BlockSpec((1, seq, width), lambda b, p, i: (b, 0, p))
    return pl.pallas_call(
        _fox_kernel,
        grid=(batch, TOK_WIDTH // width, seq // tq),
        in_specs=[
            qspec, kvspec, kvspec,
            pl.BlockSpec((1, tq, LANES), lambda b, p, i: (b, i, 0)),
            pl.BlockSpec((1, FOX_PAIRS, 2, seq), lambda b, p, i: (b, p, 0, 0)),
        ],
        out_specs=qspec,
        out_shape=jax.ShapeDtypeStruct((batch, seq, TOK_WIDTH), BF16),
        compiler_params=_params(3),
        name="fox_attention",
    )(q, k, v, c_tok, c_rows)


def _mixer_b_out_kernel(x_ref, tok_ref, qm_ref, kbd_ref, vbd_ref, wout_ref, o_ref):
    tm = x_ref.shape[1]
    blocks = [slice(r, r + MEM_ROWS) for r in range(0, tm, MEM_ROWS)]
    mem_o = _mem_attention([qm_ref[0, rows, :] for rows in blocks], kbd_ref, vbd_ref)
    mem_o = jnp.concatenate(mem_o, axis=0).astype(BF16)
    cat = jnp.concatenate([tok_ref[0], mem_o], axis=-1)
    o_ref[0] = x_ref[0] + _dot(cat, wout_ref[...])


def _mixer_b_out(x, tok, q_mem, kvbd, layer, w_out, tm):
    batch, seq, _ = x.shape

    def tile(width):
        return pl.BlockSpec((1, tm, width), lambda b, s: (b, s, 0))

    kvspec = pl.BlockSpec((1, 1, N_MEM_HEADS * N_MEM, MEM_WIDTH), lambda b, s: (layer, b, 0, 0))
    return pl.pallas_call(
        _mixer_b_out_kernel,
        grid=(batch, seq // tm),
        in_specs=[tile(D_MODEL), tile(TOK_WIDTH), tile(MEM_WIDTH), kvspec, kvspec,
                  _const_spec(w_out.shape)],
        out_specs=tile(D_MODEL),
        out_shape=jax.ShapeDtypeStruct(x.shape, F32),
        compiler_params=_params(2),
        name="mixer_b_out",
    )(x, tok, q_mem, kvbd[0], kvbd[1], w_out)


def kernel(x, mem, a_norm1, a_w_in, a_v_norm, a_w_s, a_b_s, a_mem_norm, a_w_mem_kv, a_w_out, a_norm2, a_ffn_in, a_ffn_conv, a_ffn_conv_b, a_ffn_out, kv_norm, w_kv, b_f, b_norm1, b_w_q, b_mem_norm, b_w_mem_kv, b_w_out, b_norm2, b_ffn_in, b_ffn_conv, b_ffn_conv_b, b_ffn_out, final_norm):
    n_a, n_b = a_norm1.shape[0], b_norm1.shape[0]
    assert n_a == 1 and n_b == 1, "one gMLP layer, then one forgetting-attention layer"
    batch, seq, _ = x.shape
    row = lambda a: a.reshape(1, -1)
    bf = lambda a: a.astype(BF16)

    kvbd = _memkv(mem,
                  jnp.concatenate([a_mem_norm, b_mem_norm], axis=0)[:, None, :],
                  bf(jnp.concatenate([a_w_mem_kv, b_w_mem_kv], axis=0)))

    b_full = jnp.repeat(a_b_s[0].T, TOK_WIDTH // GMLP_GROUPS, axis=1)
    x = _mixer_a(x, row(a_norm1[0]), bf(a_w_in[0]), row(a_v_norm[0]), a_w_s[0], b_full,
                 kvbd, 0, bf(a_w_out[0]), MIX_A_TM)
    x = _ffn(x, row(a_norm2[0]), bf(a_ffn_in[0]), a_ffn_conv[0], row(a_ffn_conv_b[0]),
             bf(a_ffn_out[0]), row(final_norm), FFN_TM, False)

    w_f = jnp.pad(w_kv[:, 2 * TOK_WIDTH:], ((0, 0), (0, LANES - N_FOX_HEADS)))
    b_f_pad = jnp.pad(b_f, (0, LANES - N_FOX_HEADS)).reshape(1, LANES)

    for j in range(n_b):
        k_sh, v_sh, c_tok, c_head, q_fox, q_mem = _proj_b(
            x, row(kv_norm), bf(w_kv[:, :2 * TOK_WIDTH]), bf(w_f), b_f_pad,
            row(b_norm1[j]), bf(b_w_q[j]), PROJ_TM)
        c_rows = c_head[:, :N_FOX_HEADS, :].reshape(batch, N_FOX_HEADS // 2, 2, seq)
        tok = _fox_attention(q_fox, k_sh, v_sh, c_tok, c_rows, FOX_TQ)
        x = _mixer_b_out(x, tok, q_mem, kvbd, n_a + j, bf(b_w_out[j]), MIX_B_TM)
        x = _ffn(x, row(b_norm2[j]), bf(b_ffn_in[j]), b_ffn_conv[j], row(b_ffn_conv_b[j]),
                 bf(b_ffn_out[j]), row(final_norm), FFN_TM, j == n_b - 1)
    return x
```

```python
import functools

import jax
import jax.numpy as jnp
from jax import lax
from jax.experimental import pallas as pl
from jax.experimental.pallas import tpu as pltpu

D_MODEL = 1024
N_MEM = 256
HEAD_DIM = 64
N_MEM_HEADS = 4
MEM_WIDTH = N_MEM_HEADS * HEAD_DIM
TOK_WIDTH = D_MODEL - MEM_WIDTH
N_FOX_HEADS = TOK_WIDTH // HEAD_DIM
GMLP_BLOCK = 128
GMLP_GROUPS = 4
D_FF = 2816
CONV_WIDTH = 3
EPS = 1e-6
Q_SCALE = HEAD_DIM ** -0.5

LANES = 128
SUBLANES = 8
HEAD_PAIR = 2 * HEAD_DIM
FF_CHUNK = 256
FOX_TQ = 512
FOX_PAIRS = 3
FOX_TK = 512
FFN_TM = 512
PROJ_TM = 512
MIX_A_TM = 1024
IN_CHUNK = 256
MIX_B_TM = 1024
MEMKV_B = 4
MEM_ROWS = 256
VMEM_LIMIT = 56 * 1024 * 1024

BF16 = jnp.bfloat16
F32 = jnp.float32


def _const_spec(shape):
    nd = len(shape)
    return pl.BlockSpec(shape, lambda *_: (0,) * nd, pipeline_mode=pl.Buffered(1))


def _params(n_axes):
    return pltpu.CompilerParams(
        dimension_semantics=("arbitrary",) * n_axes, vmem_limit_bytes=VMEM_LIMIT)


def _rms(x, g):
    r = lax.rsqrt(jnp.mean(x * x, axis=-1, keepdims=True) + EPS)
    return x * r * g


def _dot(a, b):
    return jnp.dot(a, b, preferred_element_type=F32)


def _dot_nt(a, b):
    return lax.dot_general(a, b, (((1,), (1,)), ((), ())), preferred_element_type=F32)


def _memkv_kernel(mem_ref, g_ref, w_ref, kbd_ref, vbd_ref):
    n_layers = w_ref.shape[0]
    feat_head = lax.broadcasted_iota(jnp.int32, (N_MEM, MEM_WIDTH), 1) // HEAD_DIM
    for b in range(mem_ref.shape[0]):
        m = mem_ref[b]
        mr = m * lax.rsqrt(jnp.mean(m * m, axis=-1, keepdims=True) + EPS)
        for l in range(n_layers):
            kv = _dot((mr * g_ref[l]).astype(BF16), w_ref[l])
            k, v = kv[:, :MEM_WIDTH], kv[:, MEM_WIDTH:]
            for h in range(N_MEM_HEADS):
                rows = slice(h * N_MEM, (h + 1) * N_MEM)
                kbd_ref[l, b, rows, :] = jnp.where(feat_head == h, k, 0.0).astype(BF16)
                vbd_ref[l, b, rows, :] = jnp.where(feat_head == h, v, 0.0).astype(BF16)


def _memkv(mem, norms, ws):
    n_layers, batch = norms.shape[0], mem.shape[0]
    out = jax.ShapeDtypeStruct((n_layers, batch, N_MEM_HEADS * N_MEM, MEM_WIDTH), BF16)
    out_spec = pl.BlockSpec((n_layers, MEMKV_B, N_MEM_HEADS * N_MEM, MEM_WIDTH),
                            lambda b: (0, b, 0, 0))
    return pl.pallas_call(
        _memkv_kernel,
        grid=(batch // MEMKV_B,),
        in_specs=[
            pl.BlockSpec((MEMKV_B, N_MEM, D_MODEL), lambda b: (b, 0, 0)),
            _const_spec(norms.shape),
            _const_spec(ws.shape),
        ],
        out_specs=[out_spec, out_spec],
        out_shape=[out, out],
        compiler_params=_params(1),
        name="memkv",
    )(mem, norms, ws)


def _mem_softmax(logits):
    probs = []
    for h in range(N_MEM_HEADS):
        seg = logits[:, h * N_MEM:(h + 1) * N_MEM]
        e = jnp.exp(seg - jnp.max(seg, axis=-1, keepdims=True))
        probs.append((e / jnp.sum(e, axis=-1, keepdims=True)).astype(BF16))
    return jnp.concatenate(probs, axis=-1)


def _mem_attention(qs, kbd_ref, vbd_ref):
    logits = [_dot_nt(q, kbd_ref[0, 0]) for q in qs]
    probs = [_mem_softmax(l) for l in logits]
    return [_dot(p, vbd_ref[0, 0]) for p in probs]


def _gmlp_mix(vn, ws_ref):
    tm = vn.shape[0]
    tri = (lax.broadcasted_iota(jnp.int32, (GMLP_BLOCK, GMLP_BLOCK), 0)
           >= lax.broadcasted_iota(jnp.int32, (GMLP_BLOCK, GMLP_BLOCK), 1))
    w = [jnp.where(tri, ws_ref[g], 0.0).astype(BF16) for g in range(GMLP_GROUPS)]
    first_half = lax.broadcasted_iota(jnp.int32, (GMLP_BLOCK, LANES), 1) < HEAD_DIM
    win_lo = (0, 128, 384, 512)
    blocks = []
    for r in range(tm // GMLP_BLOCK):
        rows = slice(r * GMLP_BLOCK, (r + 1) * GMLP_BLOCK)
        m = [_dot(w[g], vn[rows, win_lo[g]:win_lo[g] + 2 * LANES]) for g in range(GMLP_GROUPS)]
        blocks.append(jnp.concatenate([
            m[0][:, :LANES],
            jnp.where(first_half, m[0][:, LANES:], m[1][:, :LANES]),
            m[1][:, LANES:],
            m[2][:, :LANES],
            jnp.where(first_half, m[2][:, LANES:], m[3][:, :LANES]),
            m[3][:, LANES:],
        ], axis=-1))
    return jnp.concatenate(blocks, axis=0)


def _mixer_a_kernel(x_ref, g1_ref, win_ref, vg_ref, ws_ref, bs_ref, kbd_ref, vbd_ref, wout_ref,
                    o_ref):
    x = x_ref[0]
    tm = x.shape[0]
    h = _rms(x, g1_ref[...]).astype(BF16)
    n_tok = TOK_WIDTH // IN_CHUNK

    def z_chunk(c):
        return _dot(h, win_ref[:, c * IN_CHUNK:(c + 1) * IN_CHUNK])

    zv = [z_chunk(n_tok)]
    v = []
    for c in range(1, n_tok):
        zv.append(z_chunk(n_tok + c))
        v.append(jax.nn.gelu(zv[c - 1]))
    zq = z_chunk(2 * n_tok)
    v.append(jax.nn.gelu(zv[n_tok - 1]))
    zu = [z_chunk(0)]
    ssq = sum(jnp.sum(vc * vc, axis=-1, keepdims=True) for vc in v)
    r = lax.rsqrt(ssq * (1.0 / TOK_WIDTH) + EPS)
    vn = jnp.concatenate(
        [(vc * r * vg_ref[:, c * IN_CHUNK:(c + 1) * IN_CHUNK]).astype(BF16)
         for c, vc in enumerate(v)], axis=-1)
    q_mem = (zq * Q_SCALE).astype(BF16)
    blocks = [slice(r0, r0 + MEM_ROWS) for r0 in range(0, tm, MEM_ROWS)]
    logits = [_dot_nt(q_mem[rows], kbd_ref[0, 0]) for rows in blocks]
    u, probs = [], []
    for c in range(1, n_tok):
        zu.append(z_chunk(c))
        u.append(jax.nn.gelu(zu[c - 1]))
        if c - 1 < len(logits):
            probs.append(_mem_softmax(logits[c - 1]))
    probs.extend(_mem_softmax(l) for l in logits[len(probs):])
    mem_o = [_dot(p, vbd_ref[0, 0]) for p in probs]
    u.append(jax.nn.gelu(zu[n_tok - 1]))
    bias = jnp.concatenate([bs_ref[...]] * (tm // GMLP_BLOCK), axis=0)
    tok = (jnp.concatenate(u, axis=-1) * (_gmlp_mix(vn, ws_ref) + bias)).astype(BF16)
    cat = jnp.concatenate([tok, jnp.concatenate(mem_o, axis=0).astype(BF16)], axis=-1)
    o_ref[0] = x + _dot(cat, wout_ref[...])


def _mixer_a(x, g1, w_in, v_g, w_s, b_full, kvbd, layer, w_out, tm):
    batch, seq, _ = x.shape
    xspec = pl.BlockSpec((1, tm, D_MODEL), lambda b, s: (b, s, 0))
    kvspec = pl.BlockSpec((1, 1, N_MEM_HEADS * N_MEM, MEM_WIDTH), lambda b, s: (layer, b, 0, 0))
    return pl.pallas_call(
        _mixer_a_kernel,
        grid=(batch, seq // tm),
        in_specs=[
            xspec,
            _const_spec((1, D_MODEL)),
            _const_spec(w_in.shape),
            _const_spec((1, TOK_WIDTH)),
            _const_spec(w_s.shape),
            _const_spec(b_full.shape),
            kvspec, kvspec,
            _const_spec(w_out.shape),
        ],
        out_specs=xspec,
        out_shape=jax.ShapeDtypeStruct(x.shape, F32),
        compiler_params=_params(2),
        name="mixer_a",
    )(x, g1, w_in, v_g, w_s, b_full, kvbd[0], kvbd[1], w_out)


def _shift_rows(h, prev, n):
    rolled = pltpu.roll(h, n, 0)
    head = jnp.where(lax.broadcasted_iota(jnp.int32, prev.shape, 0) < n,
                     pltpu.roll(prev, n, 0), rolled[:SUBLANES])
    return jnp.concatenate([head, rolled[SUBLANES:]], axis=0)


def _ffn_kernel(x_ref, g_ref, win_ref, cw_ref, cb_ref, wout_ref, fn_ref, o_ref, carry_ref, act_ref,
                *, final_norm):
    @pl.when(pl.program_id(1) == 0)
    def _():
        carry_ref[...] = jnp.zeros_like(carry_ref)

    x = x_ref[0]
    tm = x.shape[0]
    hn = _rms(x, g_ref[...]).astype(BF16)

    def up_proj(col0):
        return _dot(hn, win_ref[:, col0:col0 + FF_CHUNK])

    def conv_cols(h, col0):
        cols = slice(col0, col0 + FF_CHUNK)
        prev = carry_ref[:, cols]
        carry_ref[:, cols] = h[tm - SUBLANES:]
        w = cw_ref[:, cols]
        return (cb_ref[:, cols] + w[2:3] * h
                + w[0:1] * _shift_rows(h, prev, 2) + w[1:2] * _shift_rows(h, prev, 1))

    for c in range(D_FF // FF_CHUNK):
        gate = conv_cols(up_proj(c * FF_CHUNK), c * FF_CHUNK)
        up = conv_cols(up_proj(D_FF + c * FF_CHUNK), D_FF + c * FF_CHUNK)
        half_gate = 0.5 * gate
        act_ref[:, c * FF_CHUNK:(c + 1) * FF_CHUNK] = (
            (half_gate + half_gate * jnp.tanh(half_gate)) * up).astype(BF16)
    y = x + _dot(act_ref[...], wout_ref[...])
    if final_norm:
        y = _rms(y, fn_ref[...])
    o_ref[0] = y


def _ffn(x, g, w_in, conv_w, conv_b, w_out, fn, tm, final_norm):
    batch, seq, _ = x.shape
    xspec = pl.BlockSpec((1, tm, D_MODEL), lambda b, s: (b, s, 0))
    return pl.pallas_call(
        functools.partial(_ffn_kernel, final_norm=final_norm),
        grid=(batch, seq // tm),
        in_specs=[
            xspec,
            _const_spec((1, D_MODEL)),
            _const_spec(w_in.shape),
            _const_spec(conv_w.shape),
            _const_spec(conv_b.shape),
            _const_spec(w_out.shape),
            _const_spec((1, D_MODEL)),
        ],
        out_specs=xspec,
        out_shape=jax.ShapeDtypeStruct(x.shape, F32),
        scratch_shapes=[pltpu.VMEM((SUBLANES, 2 * D_FF), F32), pltpu.VMEM((tm, D_FF), BF16)],
        compiler_params=_params(2),
        name="ffn_final" if final_norm else "ffn",
    )(x, g, w_in, conv_w, conv_b, w_out, fn)


def _split3(x):
    hi = x.astype(BF16)
    r = x - hi.astype(F32)
    mid = r.astype(BF16)
    lo = (r - mid.astype(F32)).astype(BF16)
    return hi, mid, lo


def _proj_b_kernel(x_ref, gkv_ref, wkv_ref, wf_ref, bf_ref, gq_ref, wq_ref,
                   k_ref, v_ref, ct_ref, cr_ref, qf_ref, qm_ref, carry_ref):
    @pl.when(pl.program_id(1) == 0)
    def _():
        carry_ref[...] = jnp.zeros_like(carry_ref)

    x = x_ref[0]
    tm = x.shape[0]
    xr = x * lax.rsqrt(jnp.mean(x * x, axis=-1, keepdims=True) + EPS)
    hkv = (xr * gkv_ref[...]).astype(BF16)
    hq = (xr * gq_ref[...]).astype(BF16)

    def project(h, w_ref, out_ref, c, col0, scale=None):
        z = _dot(h, w_ref[:, col0 + c * IN_CHUNK:col0 + (c + 1) * IN_CHUNK])
        if scale is not None:
            z = z * scale
        out_ref[0, :, c * IN_CHUNK:(c + 1) * IN_CHUNK] = z.astype(BF16)

    n_tok = TOK_WIDTH // IN_CHUNK
    f = _dot(hkv, wf_ref[...]) + bf_ref[...]
    project(hkv, wkv_ref, k_ref, 0, 0)
    log_f = jnp.minimum(f, 0.0) - jnp.log1p(jnp.exp(-jnp.abs(f)))
    tri = (lax.broadcasted_iota(jnp.int32, (tm, tm), 0)
           >= lax.broadcasted_iota(jnp.int32, (tm, tm), 1)).astype(BF16)
    hi, mid, lo = _split3(log_f)
    project(hkv, wkv_ref, k_ref, 1, 0)
    c = carry_ref[0:1, :] + ((_dot(tri, hi) + _dot(tri, mid)) + _dot(tri, lo))
    project(hkv, wkv_ref, k_ref, 2, 0)
    carry_ref[...] = jnp.broadcast_to(c[tm - 1:tm, :], carry_ref.shape)
    ct_ref[0] = c
    cr_ref[0] = c.T[:2 * SUBLANES, :]
    for j in range(n_tok):
        project(hkv, wkv_ref, v_ref, j, TOK_WIDTH)
    for j in range(n_tok):
        project(hq, wq_ref, qf_ref, j, 0, Q_SCALE)
    project(hq, wq_ref, qm_ref, 0, TOK_WIDTH, Q_SCALE)


def _proj_b(x, g_kv, w_kv, w_f, b_f, g_q, w_q, tm):
    batch, seq, _ = x.shape

    def tile(width):
        return pl.BlockSpec((1, tm, width), lambda b, s: (b, s, 0))

    return pl.pallas_call(
        _proj_b_kernel,
        grid=(batch, seq // tm),
        in_specs=[
            tile(D_MODEL),
            _const_spec((1, D_MODEL)),
            _const_spec(w_kv.shape),
            _const_spec(w_f.shape),
            _const_spec((1, LANES)),
            _const_spec((1, D_MODEL)),
            _const_spec(w_q.shape),
        ],
        out_specs=[
            tile(TOK_WIDTH), tile(TOK_WIDTH), tile(LANES),
            pl.BlockSpec((1, 2 * SUBLANES, tm), lambda b, s: (b, 0, s)),
            tile(TOK_WIDTH), tile(MEM_WIDTH),
        ],
        out_shape=[
            jax.ShapeDtypeStruct((batch, seq, TOK_WIDTH), BF16),
            jax.ShapeDtypeStruct((batch, seq, TOK_WIDTH), BF16),
            jax.ShapeDtypeStruct((batch, seq, LANES), F32),
            jax.ShapeDtypeStruct((batch, 2 * SUBLANES, seq), F32),
            jax.ShapeDtypeStruct((batch, seq, TOK_WIDTH), BF16),
            jax.ShapeDtypeStruct((batch, seq, MEM_WIDTH), BF16),
        ],
        scratch_shapes=[pltpu.VMEM((SUBLANES, LANES), F32)],
        compiler_params=_params(2),
        name="proj_b",
    )(x, g_kv, w_kv, w_f, b_f, g_q, w_q)


def _fox_tile(n_keys, group, q_ref, k_ref, v_ref, ct_ref, cr_ref, o_ref):
    tq, tk = q_ref.shape[1], FOX_TK
    n_tiles = n_keys // tk
    ct_all = ct_ref[0]
    lane = lax.broadcasted_iota(jnp.int32, (tq, LANES), 1)
    low_half = lane < HEAD_DIM
    v_low = lax.broadcasted_iota(jnp.int32, (tk, LANES), 1) < HEAD_DIM
    q_pos = lax.broadcasted_iota(jnp.int32, (tq, tk), 0) + (n_keys - tq)
    k_pos = lax.broadcasted_iota(jnp.int32, (tq, tk), 1)
    heads = range(2 * FOX_PAIRS)

    def pair_lanes(hh):
        return slice((hh // 2) * HEAD_PAIR, (hh // 2 + 1) * HEAD_PAIR)

    qh, ct = [], []
    for hh in heads:
        q2 = q_ref[0, :, pair_lanes(hh)]
        qh.append(jnp.where(low_half if hh % 2 == 0 else ~low_half, q2, jnp.zeros_like(q2)))
        head = 2 * FOX_PAIRS * group + hh
        ct.append(jnp.sum(jnp.where(lane == head, ct_all, 0.0), axis=-1, keepdims=True))

    s = [[None] * n_tiles for _ in heads]
    m_rows = [None for _ in heads]
    shift = [None for _ in heads]
    pv = [None for _ in heads]

    def pass1(hh, t):
        keys = slice(t * tk, (t + 1) * tk)
        k_t = k_ref[0, keys, pair_lanes(hh)]
        s_t = _dot_nt(qh[hh], k_t) - cr_ref[0, pl.ds(2 * FOX_PAIRS * group + hh, 1), keys]
        if (t + 1) * tk > n_keys - tq:
            s_t = jnp.where(q_pos >= k_pos + t * tk, s_t, -jnp.inf)
        s[hh][t] = s_t
        m_t = jnp.max(s_t, axis=-1, keepdims=True)
        m_rows[hh] = m_t if t == 0 else jnp.maximum(m_rows[hh], m_t)

    def pass2(hh, t):
        if t == 0:
            shift[hh] = ct[hh] - (m_rows[hh] + ct[hh])
        v_t = v_ref[0, t * tk:(t + 1) * tk, pair_lanes(hh)]
        p_t = jnp.exp(s[hh][t] + shift[hh]).astype(BF16)
        v_h = jnp.where(v_low if hh % 2 == 0 else ~v_low, v_t, jnp.ones_like(v_t))
        pv_t = _dot(p_t, v_h)
        pv[hh] = pv_t if t == 0 else pv[hh] + pv_t

    for pp in range(FOX_PAIRS + 1):
        for t in range(n_tiles):
            for hh in (2 * pp, 2 * pp + 1):
                if pp < FOX_PAIRS:
                    pass1(hh, t)
            for hh in (2 * pp - 2, 2 * pp - 1):
                if pp >= 1:
                    pass2(hh, t)
    outs = [pv[hh] / pltpu.roll(pv[hh], HEAD_DIM, 1) for hh in heads]
    for pp in range(FOX_PAIRS):
        o_ref[0, :, pp * HEAD_PAIR:(pp + 1) * HEAD_PAIR] = jnp.where(
            low_half, outs[2 * pp], outs[2 * pp + 1]).astype(BF16)


def _fox_kernel(q_ref, k_ref, v_ref, ct_ref, cr_ref, o_ref):
    tq, seq = q_ref.shape[1], k_ref.shape[1]
    group = pl.program_id(1)
    qi = pl.program_id(2)
    for i in range(seq // tq):
        pl.when(qi == i)(functools.partial(
            _fox_tile, (i + 1) * tq, group, q_ref, k_ref, v_ref, ct_ref, cr_ref, o_ref))


def _fox_attention(q, k, v, c_tok, c_rows, tq):
    batch, seq, _ = q.shape
    width = FOX_PAIRS * HEAD_PAIR
    qspec = pl.BlockSpec((1, tq, width), lambda b, p, i: (b, i, p))
    kvspec = pl.BlockSpec((1, seq, width), lambda b, p, i: (b, 0, p))
    return pl.pallas_call(
        _fox_kernel,
        grid=(batch, TOK_WIDTH // width, seq // tq),
        in_specs=[
            qspec, kvspec, kvspec,
            pl.BlockSpec((1, tq, LANES), lambda b, p, i: (b, i, 0)),
            pl.BlockSpec((1, 2 * SUBLANES, seq), lambda b, p, i: (b, 0, 0)),
        ],
        out_specs=qspec,
        out_shape=jax.ShapeDtypeStruct((batch, seq, TOK_WIDTH), BF16),
        compiler_params=_params(3),
        name="fox_attention",
    )(q, k, v, c_tok, c_rows)


def _mixer_b_out_kernel(x_ref, tok_ref, qm_ref, kbd_ref, vbd_ref, wout_ref, o_ref):
    tm = x_ref.shape[1]
    blocks = [slice(r, r + MEM_ROWS) for r in range(0, tm, MEM_ROWS)]
    mem_o = _mem_attention([qm_ref[0, rows, :] for rows in blocks], kbd_ref, vbd_ref)
    mem_o = jnp.concatenate(mem_o, axis=0).astype(BF16)
    cat = jnp.concatenate([tok_ref[0], mem_o], axis=-1)
    o_ref[0] = x_ref[0] + _dot(cat, wout_ref[...])


def _mixer_b_out(x, tok, q_mem, kvbd, layer, w_out, tm):
    batch, seq, _ = x.shape

    def tile(width):
        return pl.BlockSpec((1, tm, width), lambda b, s: (b, s, 0))

    kvspec = pl.BlockSpec((1, 1, N_MEM_HEADS * N_MEM, MEM_WIDTH), lambda b, s: (layer, b, 0, 0))
    return pl.pallas_call(
        _mixer_b_out_kernel,
        grid=(batch, seq // tm),
        in_specs=[tile(D_MODEL), tile(TOK_WIDTH), tile(MEM_WIDTH), kvspec, kvspec,
                  _const_spec(w_out.shape)],
        out_specs=tile(D_MODEL),
        out_shape=jax.ShapeDtypeStruct(x.shape, F32),
        compiler_params=_params(2),
        name="mixer_b_out",
    )(x, tok, q_mem, kvbd[0], kvbd[1], w_out)


def kernel(x, mem, a_norm1, a_w_in, a_v_norm, a_w_s, a_b_s, a_mem_norm, a_w_mem_kv, a_w_out, a_norm2, a_ffn_in, a_ffn_conv, a_ffn_conv_b, a_ffn_out, kv_norm, w_kv, b_f, b_norm1, b_w_q, b_mem_norm, b_w_mem_kv, b_w_out, b_norm2, b_ffn_in, b_ffn_conv, b_ffn_conv_b, b_ffn_out, final_norm):
    n_a, n_b = a_norm1.shape[0], b_norm1.shape[0]
    assert n_a == 1 and n_b == 1, "one gMLP layer, then one forgetting-attention layer"
    batch, seq, _ = x.shape
    row = lambda a: a.reshape(1, -1)
    bf = lambda a: a.astype(BF16)

    kvbd = _memkv(mem,
                  jnp.concatenate([a_mem_norm, b_mem_norm], axis=0)[:, None, :],
                  bf(jnp.concatenate([a_w_mem_kv, b_w_mem_kv], axis=0)))

    b_full = jnp.repeat(a_b_s[0].T, TOK_WIDTH // GMLP_GROUPS, axis=1)
    x = _mixer_a(x, row(a_norm1[0]), bf(a_w_in[0]), row(a_v_norm[0]), a_w_s[0], b_full,
                 kvbd, 0, bf(a_w_out[0]), MIX_A_TM)
    x = _ffn(x, row(a_norm2[0]), bf(a_ffn_in[0]), a_ffn_conv[0], row(a_ffn_conv_b[0]),
             bf(a_ffn_out[0]), row(final_norm), FFN_TM, False)

    w_f = jnp.pad(w_kv[:, 2 * TOK_WIDTH:], ((0, 0), (0, LANES - N_FOX_HEADS)))
    b_f_pad = jnp.pad(b_f, (0, LANES - N_FOX_HEADS)).reshape(1, LANES)

    for j in range(n_b):
        k_sh, v_sh, c_tok, c_head, q_fox, q_mem = _proj_b(
            x, row(kv_norm), bf(w_kv[:, :2 * TOK_WIDTH]), bf(w_f), b_f_pad,
            row(b_norm1[j]), bf(b_w_q[j]), PROJ_TM)
        tok = _fox_attention(q_fox, k_sh, v_sh, c_tok, c_head, FOX_TQ)
        x = _mixer_b_out(x, tok, q_mem, kvbd, n_a + j, bf(b_w_out[j]), MIX_B_TM)
        x = _ffn(x, row(b_norm2[j]), bf(b_ffn_in[j]), b_ffn_conv[j], row(b_ffn_conv_b[j]),
                 bf(b_ffn_out[j]), row(final_norm), FFN_TM, j == n_b - 1)
    return x
```

```python
import functools

import jax
import jax.numpy as jnp
from jax import lax
from jax.experimental import pallas as pl
from jax.experimental.pallas import tpu as pltpu

D_MODEL = 1024
N_MEM = 256
HEAD_DIM = 64
N_MEM_HEADS = 4
MEM_WIDTH = N_MEM_HEADS * HEAD_DIM
TOK_WIDTH = D_MODEL - MEM_WIDTH
N_FOX_HEADS = TOK_WIDTH // HEAD_DIM
GMLP_BLOCK = 128
GMLP_GROUPS = 4
D_FF = 2816
CONV_WIDTH = 3
EPS = 1e-6
Q_SCALE = HEAD_DIM ** -0.5
LOG2E = 1.4426950408889634

LANES = 128
SUBLANES = 8
HEAD_PAIR = 2 * HEAD_DIM
FF_CHUNK = 256
FOX_TQ = 512
FOX_PAIRS = 3
FOX_TK = 512
FFN_TM = 512
PROJ_TM = 512
MIX_A_TM = 1024
IN_CHUNK = 256
MIX_B_TM = 1024
MEMKV_B = 4
MEM_ROWS = 256
VMEM_LIMIT = 56 * 1024 * 1024

BF16 = jnp.bfloat16
F32 = jnp.float32


def _const_spec(shape):
    nd = len(shape)
    return pl.BlockSpec(shape, lambda *_: (0,) * nd, pipeline_mode=pl.Buffered(1))


def _params(n_axes):
    return pltpu.CompilerParams(
        dimension_semantics=("arbitrary",) * n_axes, vmem_limit_bytes=VMEM_LIMIT)


def _rms(x, g):
    r = lax.rsqrt(jnp.mean(x * x, axis=-1, keepdims=True) + EPS)
    return x * r * g


def _dot(a, b):
    return jnp.dot(a, b, preferred_element_type=F32)


def _dot_nt(a, b):
    return lax.dot_general(a, b, (((1,), (1,)), ((), ())), preferred_element_type=F32)


def _memkv_kernel(mem_ref, g_ref, w_ref, kbd_ref, vbd_ref):
    n_layers = w_ref.shape[0]
    feat_head = lax.broadcasted_iota(jnp.int32, (N_MEM, MEM_WIDTH), 1) // HEAD_DIM
    for b in range(mem_ref.shape[0]):
        m = mem_ref[b]
        mr = m * lax.rsqrt(jnp.mean(m * m, axis=-1, keepdims=True) + EPS)
        for l in range(n_layers):
            kv = _dot((mr * g_ref[l]).astype(BF16), w_ref[l])
            k, v = kv[:, :MEM_WIDTH], kv[:, MEM_WIDTH:]
            for h in range(N_MEM_HEADS):
                rows = slice(h * N_MEM, (h + 1) * N_MEM)
                kbd_ref[l, b, rows, :] = jnp.where(feat_head == h, k, 0.0).astype(BF16)
                vbd_ref[l, b, rows, :] = jnp.where(feat_head == h, v, 0.0).astype(BF16)


def _memkv(mem, norms, ws):
    n_layers, batch = norms.shape[0], mem.shape[0]
    out = jax.ShapeDtypeStruct((n_layers, batch, N_MEM_HEADS * N_MEM, MEM_WIDTH), BF16)
    out_spec = pl.BlockSpec((n_layers, MEMKV_B, N_MEM_HEADS * N_MEM, MEM_WIDTH),
                            lambda b: (0, b, 0, 0))
    return pl.pallas_call(
        _memkv_kernel,
        grid=(batch // MEMKV_B,),
        in_specs=[
            pl.BlockSpec((MEMKV_B, N_MEM, D_MODEL), lambda b: (b, 0, 0)),
            _const_spec(norms.shape),
            _const_spec(ws.shape),
        ],
        out_specs=[out_spec, out_spec],
        out_shape=[out, out],
        compiler_params=_params(1),
        name="memkv",
    )(mem, norms, ws)


def _mem_softmax(logits):
    probs = []
    for h in range(N_MEM_HEADS):
        seg = logits[:, h * N_MEM:(h + 1) * N_MEM]
        e = jnp.exp(seg - jnp.max(seg, axis=-1, keepdims=True))
        probs.append((e / jnp.sum(e, axis=-1, keepdims=True)).astype(BF16))
    return jnp.concatenate(probs, axis=-1)


def _mem_attention(qs, kbd_ref, vbd_ref):
    logits = [_dot_nt(q, kbd_ref[0, 0]) for q in qs]
    probs = [_mem_softmax(l) for l in logits]
    return [_dot(p, vbd_ref[0, 0]) for p in probs]


def _gmlp_mix(vn, ws_ref):
    tm = vn.shape[0]
    tri = (lax.broadcasted_iota(jnp.int32, (GMLP_BLOCK, GMLP_BLOCK), 0)
           >= lax.broadcasted_iota(jnp.int32, (GMLP_BLOCK, GMLP_BLOCK), 1))
    w = [jnp.where(tri, ws_ref[g], 0.0).astype(BF16) for g in range(GMLP_GROUPS)]
    first_half = lax.broadcasted_iota(jnp.int32, (GMLP_BLOCK, LANES), 1) < HEAD_DIM
    win_lo = (0, 128, 384, 512)
    blocks = []
    for r in range(tm // GMLP_BLOCK):
        rows = slice(r * GMLP_BLOCK, (r + 1) * GMLP_BLOCK)
        m = [_dot(w[g], vn[rows, win_lo[g]:win_lo[g] + 2 * LANES]) for g in range(GMLP_GROUPS)]
        blocks.append(jnp.concatenate([
            m[0][:, :LANES],
            jnp.where(first_half, m[0][:, LANES:], m[1][:, :LANES]),
            m[1][:, LANES:],
            m[2][:, :LANES],
            jnp.where(first_half, m[2][:, LANES:], m[3][:, :LANES]),
            m[3][:, LANES:],
        ], axis=-1))
    return jnp.concatenate(blocks, axis=0)


def _mixer_a_kernel(x_ref, g1_ref, win_ref, vg_ref, ws_ref, bs_ref, kbd_ref, vbd_ref, wout_ref,
                    o_ref):
    x = x_ref[0]
    tm = x.shape[0]
    h = _rms(x, g1_ref[...]).astype(BF16)
    n_tok = TOK_WIDTH // IN_CHUNK

    def z_chunk(c):
        return _dot(h, win_ref[:, c * IN_CHUNK:(c + 1) * IN_CHUNK])

    zv = [z_chunk(n_tok)]
    v = []
    for c in range(1, n_tok):
        zv.append(z_chunk(n_tok + c))
        v.append(jax.nn.gelu(zv[c - 1]))
    zq = z_chunk(2 * n_tok)
    v.append(jax.nn.gelu(zv[n_tok - 1]))
    zu = [z_chunk(0)]
    ssq = sum(jnp.sum(vc * vc, axis=-1, keepdims=True) for vc in v)
    r = lax.rsqrt(ssq * (1.0 / TOK_WIDTH) + EPS)
    vn = jnp.concatenate(
        [(vc * r * vg_ref[:, c * IN_CHUNK:(c + 1) * IN_CHUNK]).astype(BF16)
         for c, vc in enumerate(v)], axis=-1)
    q_mem = (zq * Q_SCALE).astype(BF16)
    blocks = [slice(r0, r0 + MEM_ROWS) for r0 in range(0, tm, MEM_ROWS)]
    logits = [_dot_nt(q_mem[rows], kbd_ref[0, 0]) for rows in blocks]
    u, probs = [], []
    for c in range(1, n_tok):
        zu.append(z_chunk(c))
        u.append(jax.nn.gelu(zu[c - 1]))
        if c - 1 < len(logits):
            probs.append(_mem_softmax(logits[c - 1]))
    probs.extend(_mem_softmax(l) for l in logits[len(probs):])
    mem_o = [_dot(p, vbd_ref[0, 0]) for p in probs]
    u.append(jax.nn.gelu(zu[n_tok - 1]))
    bias = jnp.concatenate([bs_ref[...]] * (tm // GMLP_BLOCK), axis=0)
    tok = (jnp.concatenate(u, axis=-1) * (_gmlp_mix(vn, ws_ref) + bias)).astype(BF16)
    cat = jnp.concatenate([tok, jnp.concatenate(mem_o, axis=0).astype(BF16)], axis=-1)
    o_ref[0] = x + _dot(cat, wout_ref[...])


def _mixer_a(x, g1, w_in, v_g, w_s, b_full, kvbd, layer, w_out, tm):
    batch, seq, _ = x.shape
    xspec = pl.BlockSpec((1, tm, D_MODEL), lambda b, s: (b, s, 0))
    kvspec = pl.BlockSpec((1, 1, N_MEM_HEADS * N_MEM, MEM_WIDTH), lambda b, s: (layer, b, 0, 0))
    return pl.pallas_call(
        _mixer_a_kernel,
        grid=(batch, seq // tm),
        in_specs=[
            xspec,
            _const_spec((1, D_MODEL)),
            _const_spec(w_in.shape),
            _const_spec((1, TOK_WIDTH)),
            _const_spec(w_s.shape),
            _const_spec(b_full.shape),
            kvspec, kvspec,
            _const_spec(w_out.shape),
        ],
        out_specs=xspec,
        out_shape=jax.ShapeDtypeStruct(x.shape, F32),
        compiler_params=_params(2),
        name="mixer_a",
    )(x, g1, w_in, v_g, w_s, b_full, kvbd[0], kvbd[1], w_out)


def _shift_rows(h, prev, n):
    rolled = pltpu.roll(h, n, 0)
    head = jnp.where(lax.broadcasted_iota(jnp.int32, prev.shape, 0) < n,
                     pltpu.roll(prev, n, 0), rolled[:SUBLANES])
    return jnp.concatenate([head, rolled[SUBLANES:]], axis=0)


def _ffn_kernel(x_ref, g_ref, win_ref, cw_ref, cb_ref, wout_ref, fn_ref, o_ref, carry_ref, act_ref,
                *, final_norm):
    @pl.when(pl.program_id(1) == 0)
    def _():
        carry_ref[...] = jnp.zeros_like(carry_ref)

    x = x_ref[0]
    tm = x.shape[0]
    hn = _rms(x, g_ref[...]).astype(BF16)

    def up_proj(col0):
        return _dot(hn, win_ref[:, col0:col0 + FF_CHUNK])

    def conv_cols(h, col0):
        cols = slice(col0, col0 + FF_CHUNK)
        prev = carry_ref[:, cols]
        carry_ref[:, cols] = h[tm - SUBLANES:]
        w = cw_ref[:, cols]
        return (cb_ref[:, cols] + w[2:3] * h
                + w[0:1] * _shift_rows(h, prev, 2) + w[1:2] * _shift_rows(h, prev, 1))

    for c in range(D_FF // FF_CHUNK):
        gate = conv_cols(up_proj(c * FF_CHUNK), c * FF_CHUNK)
        up = conv_cols(up_proj(D_FF + c * FF_CHUNK), D_FF + c * FF_CHUNK)
        half_gate = 0.5 * gate
        act_ref[:, c * FF_CHUNK:(c + 1) * FF_CHUNK] = (
            (half_gate + half_gate * jnp.tanh(half_gate)) * up).astype(BF16)
    y = x + _dot(act_ref[...], wout_ref[...])
    if final_norm:
        y = _rms(y, fn_ref[...])
    o_ref[0] = y


def _ffn(x, g, w_in, conv_w, conv_b, w_out, fn, tm, final_norm):
    batch, seq, _ = x.shape
    xspec = pl.BlockSpec((1, tm, D_MODEL), lambda b, s: (b, s, 0))
    return pl.pallas_call(
        functools.partial(_ffn_kernel, final_norm=final_norm),
        grid=(batch, seq // tm),
        in_specs=[
            xspec,
            _const_spec((1, D_MODEL)),
            _const_spec(w_in.shape),
            _const_spec(conv_w.shape),
            _const_spec(conv_b.shape),
            _const_spec(w_out.shape),
            _const_spec((1, D_MODEL)),
        ],
        out_specs=xspec,
        out_shape=jax.ShapeDtypeStruct(x.shape, F32),
        scratch_shapes=[pltpu.VMEM((SUBLANES, 2 * D_FF), F32), pltpu.VMEM((tm, D_FF), BF16)],
        compiler_params=_params(2),
        name="ffn_final" if final_norm else "ffn",
    )(x, g, w_in, conv_w, conv_b, w_out, fn)


def _split3(x):
    hi = x.astype(BF16)
    r = x - hi.astype(F32)
    mid = r.astype(BF16)
    lo = (r - mid.astype(F32)).astype(BF16)
    return hi, mid, lo


def _proj_b_kernel(x_ref, gkv_ref, wkv_ref, wf_ref, bf_ref, gq_ref, wq_ref,
                   k_ref, v_ref, ct_ref, cr_ref, qf_ref, qm_ref, carry_ref):
    @pl.when(pl.program_id(1) == 0)
    def _():
        carry_ref[...] = jnp.zeros_like(carry_ref)

    x = x_ref[0]
    tm = x.shape[0]
    xr = x * lax.rsqrt(jnp.mean(x * x, axis=-1, keepdims=True) + EPS)
    hkv = (xr * gkv_ref[...]).astype(BF16)
    hq = (xr * gq_ref[...]).astype(BF16)

    def project(h, w_ref, out_ref, c, col0, scale=None):
        z = _dot(h, w_ref[:, col0 + c * IN_CHUNK:col0 + (c + 1) * IN_CHUNK])
        if scale is not None:
            z = z * scale
        out_ref[0, :, c * IN_CHUNK:(c + 1) * IN_CHUNK] = z.astype(BF16)

    n_tok = TOK_WIDTH // IN_CHUNK
    f = _dot(hkv, wf_ref[...]) + bf_ref[...]
    project(hkv, wkv_ref, k_ref, 0, 0)
    log_f = jnp.minimum(f, 0.0) - jnp.log1p(jnp.exp(-jnp.abs(f)))
    tri = (lax.broadcasted_iota(jnp.int32, (tm, tm), 0)
           >= lax.broadcasted_iota(jnp.int32, (tm, tm), 1)).astype(BF16)
    hi, mid, lo = _split3(log_f)
    project(hkv, wkv_ref, k_ref, 1, 0)
    c = carry_ref[0:1, :] + ((_dot(tri, hi) + _dot(tri, mid)) + _dot(tri, lo))
    project(hkv, wkv_ref, k_ref, 2, 0)
    carry_ref[...] = jnp.broadcast_to(c[tm - 1:tm, :], carry_ref.shape)
    c2 = c * LOG2E
    ct_ref[0] = c2
    cr_ref[0] = c2.T[:2 * SUBLANES, :]
    for j in range(n_tok):
        project(hkv, wkv_ref, v_ref, j, TOK_WIDTH)
    for j in range(n_tok):
        project(hq, wq_ref, qf_ref, j, 0, Q_SCALE * LOG2E)
    project(hq, wq_ref, qm_ref, 0, TOK_WIDTH, Q_SCALE)


def _proj_b(x, g_kv, w_kv, w_f, b_f, g_q, w_q, tm):
    batch, seq, _ = x.shape

    def tile(width):
        return pl.BlockSpec((1, tm, width), lambda b, s: (b, s, 0))

    return pl.pallas_call(
        _proj_b_kernel,
        grid=(batch, seq // tm),
        in_specs=[
            tile(D_MODEL),
            _const_spec((1, D_MODEL)),
            _const_spec(w_kv.shape),
            _const_spec(w_f.shape),
            _const_spec((1, LANES)),
            _const_spec((1, D_MODEL)),
            _const_spec(w_q.shape),
        ],
        out_specs=[
            tile(TOK_WIDTH), tile(TOK_WIDTH), tile(LANES),
            pl.BlockSpec((1, 2 * SUBLANES, tm), lambda b, s: (b, 0, s)),
            tile(TOK_WIDTH), tile(MEM_WIDTH),
        ],
        out_shape=[
            jax.ShapeDtypeStruct((batch, seq, TOK_WIDTH), BF16),
            jax.ShapeDtypeStruct((batch, seq, TOK_WIDTH), BF16),
            jax.ShapeDtypeStruct((batch, seq, LANES), F32),
            jax.ShapeDtypeStruct((batch, 2 * SUBLANES, seq), F32),
            jax.ShapeDtypeStruct((batch, seq, TOK_WIDTH), BF16),
            jax.ShapeDtypeStruct((batch, seq, MEM_WIDTH), BF16),
        ],
        scratch_shapes=[pltpu.VMEM((SUBLANES, LANES), F32)],
        compiler_params=_params(2),
        name="proj_b",
    )(x, g_kv, w_kv, w_f, b_f, g_q, w_q)


def _fox_tile(n_keys, group, q_ref, k_ref, v_ref, ct_ref, cr_ref, o_ref):
    tq, tk = q_ref.shape[1], FOX_TK
    n_tiles = n_keys // tk
    ct_all = ct_ref[0]
    lane = lax.broadcasted_iota(jnp.int32, (tq, LANES), 1)
    low_half = lane < HEAD_DIM
    v_low = lax.broadcasted_iota(jnp.int32, (tk, LANES), 1) < HEAD_DIM
    q_pos = lax.broadcasted_iota(jnp.int32, (tq, tk), 0) + (n_keys - tq)
    k_pos = lax.broadcasted_iota(jnp.int32, (tq, tk), 1)
    heads = range(2 * FOX_PAIRS)

    def pair_lanes(hh):
        return slice((hh // 2) * HEAD_PAIR, (hh // 2 + 1) * HEAD_PAIR)

    qh, ct = [], []
    for hh in heads:
        q2 = q_ref[0, :, pair_lanes(hh)]
        qh.append(jnp.where(low_half if hh % 2 == 0 else ~low_half, q2, jnp.zeros_like(q2)))
        head = 2 * FOX_PAIRS * group + hh
        ct.append(jnp.sum(jnp.where(lane == head, ct_all, 0.0), axis=-1, keepdims=True))

    s = [[None] * n_tiles for _ in heads]
    m_rows = [None for _ in heads]
    shift = [None for _ in heads]
    pv = [None for _ in heads]

    def pass1(hh, t):
        keys = slice(t * tk, (t + 1) * tk)
        k_t = k_ref[0, keys, pair_lanes(hh)]
        s_t = _dot_nt(qh[hh], k_t) - cr_ref[0, pl.ds(2 * FOX_PAIRS * group + hh, 1), keys]
        if (t + 1) * tk > n_keys - tq:
            s_t = jnp.where(q_pos >= k_pos + t * tk, s_t, -jnp.inf)
        s[hh][t] = s_t
        m_t = jnp.max(s_t, axis=-1, keepdims=True)
        m_rows[hh] = m_t if t == 0 else jnp.maximum(m_rows[hh], m_t)

    def pass2(hh, t):
        if t == 0:
            shift[hh] = ct[hh] - (m_rows[hh] + ct[hh])
        v_t = v_ref[0, t * tk:(t + 1) * tk, pair_lanes(hh)]
        p_t = jnp.exp2(s[hh][t] + shift[hh]).astype(BF16)
        v_h = jnp.where(v_low if hh % 2 == 0 else ~v_low, v_t, jnp.ones_like(v_t))
        pv_t = _dot(p_t, v_h)
        pv[hh] = pv_t if t == 0 else pv[hh] + pv_t

    for pp in range(FOX_PAIRS + 1):
        for t in range(n_tiles):
            for hh in (2 * pp, 2 * pp + 1):
                if pp < FOX_PAIRS:
                    pass1(hh, t)
            for hh in (2 * pp - 2, 2 * pp - 1):
                if pp >= 1:
                    pass2(hh, t)
    outs = [pv[hh] / pltpu.roll(pv[hh], HEAD_DIM, 1) for hh in heads]
    for pp in range(FOX_PAIRS):
        o_ref[0, :, pp * HEAD_PAIR:(pp + 1) * HEAD_PAIR] = jnp.where(
            low_half, outs[2 * pp], outs[2 * pp + 1]).astype(BF16)


def _fox_kernel(q_ref, k_ref, v_ref, ct_ref, cr_ref, o_ref):
    tq, seq = q_ref.shape[1], k_ref.shape[1]
    group = pl.program_id(1)
    qi = pl.program_id(2)
    for i in range(seq // tq):
        pl.when(qi == i)(functools.partial(
            _fox_tile, (i + 1) * tq, group, q_ref, k_ref, v_ref, ct_ref, cr_ref, o_ref))


def _fox_attention(q, k, v, c_tok, c_rows, tq):
    batch, seq, _ = q.shape
    width = FOX_PAIRS * HEAD_PAIR
    qspec = pl.BlockSpec((1, tq, width), lambda b, p, i: (b, i, p))
    kvspec = pl.BlockSpec((1, seq, width), lambda b, p, i: (b, 0, p))
    return pl.pallas_call(
        _fox_kernel,
        grid=(batch, TOK_WIDTH // width, seq // tq),
        in_specs=[
            qspec, kvspec, kvspec,
            pl.BlockSpec((1, tq, LANES), lambda b, p, i: (b, i, 0)),
            pl.BlockSpec((1, 2 * SUBLANES, seq), lambda b, p, i: (b, 0, 0)),
        ],
        out_specs=qspec,
        out_shape=jax.ShapeDtypeStruct((batch, seq, TOK_WIDTH), BF16),
        compiler_params=_params(3),
        name="fox_attention",
    )(q, k, v, c_tok, c_rows)


def _mixer_b_out_kernel(x_ref, tok_ref, qm_ref, kbd_ref, vbd_ref, wout_ref, o_ref):
    tm = x_ref.shape[1]
    blocks = [slice(r, r + MEM_ROWS) for r in range(0, tm, MEM_ROWS)]
    mem_o = _mem_attention([qm_ref[0, rows, :] for rows in blocks], kbd_ref, vbd_ref)
    mem_o = jnp.concatenate(mem_o, axis=0).astype(BF16)
    cat = jnp.concatenate([tok_ref[0], mem_o], axis=-1)
    o_ref[0] = x_ref[0] + _dot(cat, wout_ref[...])


def _mixer_b_out(x, tok, q_mem, kvbd, layer, w_out, tm):
    batch, seq, _ = x.shape

    def tile(width):
        return pl.BlockSpec((1, tm, width), lambda b, s: (b, s, 0))

    kvspec = pl.BlockSpec((1, 1, N_MEM_HEADS * N_MEM, MEM_WIDTH), lambda b, s: (layer, b, 0, 0))
    return pl.pallas_call(
        _mixer_b_out_kernel,
        grid=(batch, seq // tm),
        in_specs=[tile(D_MODEL), tile(TOK_WIDTH), tile(MEM_WIDTH), kvspec, kvspec,
                  _const_spec(w_out.shape)],
        out_specs=tile(D_MODEL),
        out_shape=jax.ShapeDtypeStruct(x.shape, F32),
        compiler_params=_params(2),
        name="mixer_b_out",
    )(x, tok, q_mem, kvbd[0], kvbd[1], w_out)


def kernel(x, mem, a_norm1, a_w_in, a_v_norm, a_w_s, a_b_s, a_mem_norm, a_w_mem_kv, a_w_out, a_norm2, a_ffn_in, a_ffn_conv, a_ffn_conv_b, a_ffn_out, kv_norm, w_kv, b_f, b_norm1, b_w_q, b_mem_norm, b_w_mem_kv, b_w_out, b_norm2, b_ffn_in, b_ffn_conv, b_ffn_conv_b, b_ffn_out, final_norm):
    n_a, n_b = a_norm1.shape[0], b_norm1.shape[0]
    assert n_a == 1 and n_b == 1, "one gMLP layer, then one forgetting-attention layer"
    batch, seq, _ = x.shape
    row = lambda a: a.reshape(1, -1)
    bf = lambda a: a.astype(BF16)

    kvbd = _memkv(mem,
                  jnp.concatenate([a_mem_norm, b_mem_norm], axis=0)[:, None, :],
                  bf(jnp.concatenate([a_w_mem_kv, b_w_mem_kv], axis=0)))

    b_full = jnp.repeat(a_b_s[0].T, TOK_WIDTH // GMLP_GROUPS, axis=1)
    x = _mixer_a(x, row(a_norm1[0]), bf(a_w_in[0]), row(a_v_norm[0]), a_w_s[0], b_full,
                 kvbd, 0, bf(a_w_out[0]), MIX_A_TM)
    x = _ffn(x, row(a_norm2[0]), bf(a_ffn_in[0]), a_ffn_conv[0], row(a_ffn_conv_b[0]),
             bf(a_ffn_out[0]), row(final_norm), FFN_TM, False)

    w_f = jnp.pad(w_kv[:, 2 * TOK_WIDTH:], ((0, 0), (0, LANES - N_FOX_HEADS)))
    b_f_pad = jnp.pad(b_f, (0, LANES - N_FOX_HEADS)).reshape(1, LANES)

    for j in range(n_b):
        k_sh, v_sh, c_tok, c_head, q_fox, q_mem = _proj_b(
            x, row(kv_norm), bf(w_kv[:, :2 * TOK_WIDTH]), bf(w_f), b_f_pad,
            row(b_norm1[j]), bf(b_w_q[j]), PROJ_TM)
        tok = _fox_attention(q_fox, k_sh, v_sh, c_tok, c_head, FOX_TQ)
        x = _mixer_b_out(x, tok, q_mem, kvbd, n_a + j, bf(b_w_out[j]), MIX_B_TM)
        x = _ffn(x, row(b_norm2[j]), bf(b_ffn_in[j]), b_ffn_conv[j], row(b_ffn_conv_b[j]),
                 bf(b_ffn_out[j]), row(final_norm), FFN_TM, j == n_b - 1)
    return x
```

```python
import functools

import jax
import jax.numpy as jnp
from jax import lax
from jax.experimental import pallas as pl
from jax.experimental.pallas import tpu as pltpu

D_MODEL = 1024
N_MEM = 256
HEAD_DIM = 64
N_MEM_HEADS = 4
MEM_WIDTH = N_MEM_HEADS * HEAD_DIM
TOK_WIDTH = D_MODEL - MEM_WIDTH
N_FOX_HEADS = TOK_WIDTH // HEAD_DIM
GMLP_BLOCK = 128
GMLP_GROUPS = 4
D_FF = 2816
CONV_WIDTH = 3
EPS = 1e-6
Q_SCALE = HEAD_DIM ** -0.5

LANES = 128
SUBLANES = 8
HEAD_PAIR = 2 * HEAD_DIM
FF_CHUNK = 256
FOX_TQ = 512
FOX_PAIRS = 3
FOX_TK = 512
FFN_TM = 512
CUMSUM_ROWS = 256
PROJ_TM = 1024
MIX_A_TM = 1024
IN_CHUNK = 256
MIX_B_TM = 1024
MEMKV_B = 4
MEM_ROWS = 256
VMEM_LIMIT = 56 * 1024 * 1024

BF16 = jnp.bfloat16
F32 = jnp.float32


def _const_spec(shape):
    nd = len(shape)
    return pl.BlockSpec(shape, lambda *_: (0,) * nd, pipeline_mode=pl.Buffered(1))


def _params(n_axes):
    return pltpu.CompilerParams(
        dimension_semantics=("arbitrary",) * n_axes, vmem_limit_bytes=VMEM_LIMIT)


def _rms(x, g):
    r = lax.rsqrt(jnp.mean(x * x, axis=-1, keepdims=True) + EPS)
    return x * r * g


def _dot(a, b):
    return jnp.dot(a, b, preferred_element_type=F32)


def _dot_nt(a, b):
    return lax.dot_general(a, b, (((1,), (1,)), ((), ())), preferred_element_type=F32)


def _memkv_kernel(mem_ref, g_ref, w_ref, kbd_ref, vbd_ref):
    n_layers = w_ref.shape[0]
    feat_head = lax.broadcasted_iota(jnp.int32, (N_MEM, MEM_WIDTH), 1) // HEAD_DIM
    for b in range(mem_ref.shape[0]):
        m = mem_ref[b]
        mr = m * lax.rsqrt(jnp.mean(m * m, axis=-1, keepdims=True) + EPS)
        for l in range(n_layers):
            kv = _dot((mr * g_ref[l]).astype(BF16), w_ref[l])
            k, v = kv[:, :MEM_WIDTH], kv[:, MEM_WIDTH:]
            for h in range(N_MEM_HEADS):
                rows = slice(h * N_MEM, (h + 1) * N_MEM)
                kbd_ref[l, b, rows, :] = jnp.where(feat_head == h, k, 0.0).astype(BF16)
                vbd_ref[l, b, rows, :] = jnp.where(feat_head == h, v, 0.0).astype(BF16)


def _memkv(mem, norms, ws):
    n_layers, batch = norms.shape[0], mem.shape[0]
    out = jax.ShapeDtypeStruct((n_layers, batch, N_MEM_HEADS * N_MEM, MEM_WIDTH), BF16)
    out_spec = pl.BlockSpec((n_layers, MEMKV_B, N_MEM_HEADS * N_MEM, MEM_WIDTH),
                            lambda b: (0, b, 0, 0))
    return pl.pallas_call(
        _memkv_kernel,
        grid=(batch // MEMKV_B,),
        in_specs=[
            pl.BlockSpec((MEMKV_B, N_MEM, D_MODEL), lambda b: (b, 0, 0)),
            _const_spec(norms.shape),
            _const_spec(ws.shape),
        ],
        out_specs=[out_spec, out_spec],
        out_shape=[out, out],
        compiler_params=_params(1),
        name="memkv",
    )(mem, norms, ws)


def _mem_softmax(logits):
    probs = []
    for h in range(N_MEM_HEADS):
        seg = logits[:, h * N_MEM:(h + 1) * N_MEM]
        e = jnp.exp(seg - jnp.max(seg, axis=-1, keepdims=True))
        probs.append((e / jnp.sum(e, axis=-1, keepdims=True)).astype(BF16))
    return jnp.concatenate(probs, axis=-1)


def _mem_attention(qs, kbd_ref, vbd_ref):
    logits = [_dot_nt(q, kbd_ref[0, 0]) for q in qs]
    probs = [_mem_softmax(l) for l in logits]
    return [_dot(p, vbd_ref[0, 0]) for p in probs]


def _gmlp_mix(vn, ws_ref):
    tm = vn.shape[0]
    tri = (lax.broadcasted_iota(jnp.int32, (GMLP_BLOCK, GMLP_BLOCK), 0)
           >= lax.broadcasted_iota(jnp.int32, (GMLP_BLOCK, GMLP_BLOCK), 1))
    w = [jnp.where(tri, ws_ref[g], 0.0).astype(BF16) for g in range(GMLP_GROUPS)]
    first_half = lax.broadcasted_iota(jnp.int32, (GMLP_BLOCK, LANES), 1) < HEAD_DIM
    win_lo = (0, 128, 384, 512)
    blocks = []
    for r in range(tm // GMLP_BLOCK):
        rows = slice(r * GMLP_BLOCK, (r + 1) * GMLP_BLOCK)
        m = [_dot(w[g], vn[rows, win_lo[g]:win_lo[g] + 2 * LANES]) for g in range(GMLP_GROUPS)]
        blocks.append(jnp.concatenate([
            m[0][:, :LANES],
            jnp.where(first_half, m[0][:, LANES:], m[1][:, :LANES]),
            m[1][:, LANES:],
            m[2][:, :LANES],
            jnp.where(first_half, m[2][:, LANES:], m[3][:, :LANES]),
            m[3][:, LANES:],
        ], axis=-1))
    return jnp.concatenate(blocks, axis=0)


def _mixer_a_kernel(x_ref, g1_ref, win_ref, vg_ref, ws_ref, bs_ref, kbd_ref, vbd_ref, wout_ref,
                    o_ref):
    x = x_ref[0]
    tm = x.shape[0]
    h = _rms(x, g1_ref[...]).astype(BF16)
    n_tok = TOK_WIDTH // IN_CHUNK

    def z_chunk(c):
        return _dot(h, win_ref[:, c * IN_CHUNK:(c + 1) * IN_CHUNK])

    zv = [z_chunk(n_tok)]
    v = []
    for c in range(1, n_tok):
        zv.append(z_chunk(n_tok + c))
        v.append(jax.nn.gelu(zv[c - 1]))
    zq = z_chunk(2 * n_tok)
    v.append(jax.nn.gelu(zv[n_tok - 1]))
    zu = [z_chunk(0)]
    ssq = sum(jnp.sum(vc * vc, axis=-1, keepdims=True) for vc in v)
    r = lax.rsqrt(ssq * (1.0 / TOK_WIDTH) + EPS)
    vn = jnp.concatenate(
        [(vc * r * vg_ref[:, c * IN_CHUNK:(c + 1) * IN_CHUNK]).astype(BF16)
         for c, vc in enumerate(v)], axis=-1)
    q_mem = (zq * Q_SCALE).astype(BF16)
    blocks = [slice(r0, r0 + MEM_ROWS) for r0 in range(0, tm, MEM_ROWS)]
    logits = [_dot_nt(q_mem[rows], kbd_ref[0, 0]) for rows in blocks]
    u, probs = [], []
    for c in range(1, n_tok):
        zu.append(z_chunk(c))
        u.append(jax.nn.gelu(zu[c - 1]))
        if c - 1 < len(logits):
            probs.append(_mem_softmax(logits[c - 1]))
    probs.extend(_mem_softmax(l) for l in logits[len(probs):])
    mem_o = [_dot(p, vbd_ref[0, 0]) for p in probs]
    u.append(jax.nn.gelu(zu[n_tok - 1]))
    bias = jnp.concatenate([bs_ref[...]] * (tm // GMLP_BLOCK), axis=0)
    tok = (jnp.concatenate(u, axis=-1) * (_gmlp_mix(vn, ws_ref) + bias)).astype(BF16)
    cat = jnp.concatenate([tok, jnp.concatenate(mem_o, axis=0).astype(BF16)], axis=-1)
    o_ref[0] = x + _dot(cat, wout_ref[...])


def _mixer_a(x, g1, w_in, v_g, w_s, b_full, kvbd, layer, w_out, tm):
    batch, seq, _ = x.shape
    xspec = pl.BlockSpec((1, tm, D_MODEL), lambda b, s: (b, s, 0))
    kvspec = pl.BlockSpec((1, 1, N_MEM_HEADS * N_MEM, MEM_WIDTH), lambda b, s: (layer, b, 0, 0))
    return pl.pallas_call(
        _mixer_a_kernel,
        grid=(batch, seq // tm),
        in_specs=[
            xspec,
            _const_spec((1, D_MODEL)),
            _const_spec(w_in.shape),
            _const_spec((1, TOK_WIDTH)),
            _const_spec(w_s.shape),
            _const_spec(b_full.shape),
            kvspec, kvspec,
            _const_spec(w_out.shape),
        ],
        out_specs=xspec,
        out_shape=jax.ShapeDtypeStruct(x.shape, F32),
        compiler_params=_params(2),
        name="mixer_a",
    )(x, g1, w_in, v_g, w_s, b_full, kvbd[0], kvbd[1], w_out)


def _shift_rows(h, prev, n):
    rolled = pltpu.roll(h, n, 0)
    head = jnp.where(lax.broadcasted_iota(jnp.int32, prev.shape, 0) < n,
                     pltpu.roll(prev, n, 0), rolled[:SUBLANES])
    return jnp.concatenate([head, rolled[SUBLANES:]], axis=0)


def _ffn_kernel(x_ref, g_ref, win_ref, cw_ref, cb_ref, wout_ref, fn_ref, o_ref, carry_ref, act_ref,
                *, final_norm):
    @pl.when(pl.program_id(1) == 0)
    def _():
        carry_ref[...] = jnp.zeros_like(carry_ref)

    x = x_ref[0]
    tm = x.shape[0]
    hn = _rms(x, g_ref[...]).astype(BF16)

    def up_proj(col0):
        return _dot(hn, win_ref[:, col0:col0 + FF_CHUNK])

    def conv_cols(h, col0):
        cols = slice(col0, col0 + FF_CHUNK)
        prev = carry_ref[:, cols]
        carry_ref[:, cols] = h[tm - SUBLANES:]
        w = cw_ref[:, cols]
        return (cb_ref[:, cols] + w[2:3] * h
                + w[0:1] * _shift_rows(h, prev, 2) + w[1:2] * _shift_rows(h, prev, 1))

    for c in range(D_FF // FF_CHUNK):
        gate = conv_cols(up_proj(c * FF_CHUNK), c * FF_CHUNK)
        up = conv_cols(up_proj(D_FF + c * FF_CHUNK), D_FF + c * FF_CHUNK)
        half_gate = 0.5 * gate
        act_ref[:, c * FF_CHUNK:(c + 1) * FF_CHUNK] = (
            (half_gate + half_gate * jnp.tanh(half_gate)) * up).astype(BF16)
    y = x + _dot(act_ref[...], wout_ref[...])
    if final_norm:
        y = _rms(y, fn_ref[...])
    o_ref[0] = y


def _ffn(x, g, w_in, conv_w, conv_b, w_out, fn, tm, final_norm):
    batch, seq, _ = x.shape
    xspec = pl.BlockSpec((1, tm, D_MODEL), lambda b, s: (b, s, 0))
    return pl.pallas_call(
        functools.partial(_ffn_kernel, final_norm=final_norm),
        grid=(batch, seq // tm),
        in_specs=[
            xspec,
            _const_spec((1, D_MODEL)),
            _const_spec(w_in.shape),
            _const_spec(conv_w.shape),
            _const_spec(conv_b.shape),
            _const_spec(w_out.shape),
            _const_spec((1, D_MODEL)),
        ],
        out_specs=xspec,
        out_shape=jax.ShapeDtypeStruct(x.shape, F32),
        scratch_shapes=[pltpu.VMEM((SUBLANES, 2 * D_FF), F32), pltpu.VMEM((tm, D_FF), BF16)],
        compiler_params=_params(2),
        name="ffn_final" if final_norm else "ffn",
    )(x, g, w_in, conv_w, conv_b, w_out, fn)


def _split3(x):
    hi = x.astype(BF16)
    r = x - hi.astype(F32)
    mid = r.astype(BF16)
    lo = (r - mid.astype(F32)).astype(BF16)
    return hi, mid, lo


def _proj_b_kernel(x_ref, gkv_ref, wkv_ref, wf_ref, bf_ref, gq_ref, wq_ref,
                   k_ref, v_ref, ct_ref, cr_ref, qf_ref, qm_ref, carry_ref):
    @pl.when(pl.program_id(1) == 0)
    def _():
        carry_ref[...] = jnp.zeros_like(carry_ref)

    x = x_ref[0]
    tm = x.shape[0]
    xr = x * lax.rsqrt(jnp.mean(x * x, axis=-1, keepdims=True) + EPS)
    hkv = (xr * gkv_ref[...]).astype(BF16)
    hq = (xr * gq_ref[...]).astype(BF16)

    def project(h, w_ref, out_ref, c, col0, scale=None):
        z = _dot(h, w_ref[:, col0 + c * IN_CHUNK:col0 + (c + 1) * IN_CHUNK])
        if scale is not None:
            z = z * scale
        out_ref[0, :, c * IN_CHUNK:(c + 1) * IN_CHUNK] = z.astype(BF16)

    n_tok = TOK_WIDTH // IN_CHUNK
    f = _dot(hkv, wf_ref[...]) + bf_ref[...]
    project(hkv, wkv_ref, k_ref, 0, 0)
    log_f = jnp.minimum(f, 0.0) - jnp.log1p(jnp.exp(-jnp.abs(f)))
    tri = (lax.broadcasted_iota(jnp.int32, (CUMSUM_ROWS, CUMSUM_ROWS), 0)
           >= lax.broadcasted_iota(jnp.int32, (CUMSUM_ROWS, CUMSUM_ROWS), 1)).astype(BF16)
    parts = _split3(log_f)
    project(hkv, wkv_ref, k_ref, 1, 0)
    carry, c_blocks = carry_ref[0:1, :], []
    for r0 in range(0, tm, CUMSUM_ROWS):
        hi, mid, lo = (part[r0:r0 + CUMSUM_ROWS] for part in parts)
        c_blk = carry + ((_dot(tri, hi) + _dot(tri, mid)) + _dot(tri, lo))
        carry = c_blk[CUMSUM_ROWS - 1:CUMSUM_ROWS, :]
        c_blocks.append(c_blk)
    c = jnp.concatenate(c_blocks, axis=0)
    project(hkv, wkv_ref, k_ref, 2, 0)
    carry_ref[...] = jnp.broadcast_to(carry, carry_ref.shape)
    ct_ref[0] = c
    cr_ref[0] = c.T[:2 * SUBLANES, :]
    for j in range(n_tok):
        project(hkv, wkv_ref, v_ref, j, TOK_WIDTH)
    for j in range(n_tok):
        project(hq, wq_ref, qf_ref, j, 0, Q_SCALE)
    project(hq, wq_ref, qm_ref, 0, TOK_WIDTH, Q_SCALE)


def _proj_b(x, g_kv, w_kv, w_f, b_f, g_q, w_q, tm):
    batch, seq, _ = x.shape

    def tile(width):
        return pl.BlockSpec((1, tm, width), lambda b, s: (b, s, 0))

    return pl.pallas_call(
        _proj_b_kernel,
        grid=(batch, seq // tm),
        in_specs=[
            tile(D_MODEL),
            _const_spec((1, D_MODEL)),
            _const_spec(w_kv.shape),
            _const_spec(w_f.shape),
            _const_spec((1, LANES)),
            _const_spec((1, D_MODEL)),
            _const_spec(w_q.shape),
        ],
        out_specs=[
            tile(TOK_WIDTH), tile(TOK_WIDTH), tile(LANES),
            pl.BlockSpec((1, 2 * SUBLANES, tm), lambda b, s: (b, 0, s)),
            tile(TOK_WIDTH), tile(MEM_WIDTH),
        ],
        out_shape=[
            jax.ShapeDtypeStruct((batch, seq, TOK_WIDTH), BF16),
            jax.ShapeDtypeStruct((batch, seq, TOK_WIDTH), BF16),
            jax.ShapeDtypeStruct((batch, seq, LANES), F32),
            jax.ShapeDtypeStruct((batch, 2 * SUBLANES, seq), F32),
            jax.ShapeDtypeStruct((batch, seq, TOK_WIDTH), BF16),
            jax.ShapeDtypeStruct((batch, seq, MEM_WIDTH), BF16),
        ],
        scratch_shapes=[pltpu.VMEM((SUBLANES, LANES), F32)],
        compiler_params=_params(2),
        name="proj_b",
    )(x, g_kv, w_kv, w_f, b_f, g_q, w_q)


def _fox_tile(n_keys, group, q_ref, k_ref, v_ref, ct_ref, cr_ref, o_ref):
    tq, tk = q_ref.shape[1], FOX_TK
    n_tiles = n_keys // tk
    ct_all = ct_ref[0]
    lane = lax.broadcasted_iota(jnp.int32, (tq, LANES), 1)
    low_half = lane < HEAD_DIM
    v_low = lax.broadcasted_iota(jnp.int32, (tk, LANES), 1) < HEAD_DIM
    q_pos = lax.broadcasted_iota(jnp.int32, (tq, tk), 0) + (n_keys - tq)
    k_pos = lax.broadcasted_iota(jnp.int32, (tq, tk), 1)
    heads = range(2 * FOX_PAIRS)

    def pair_lanes(hh):
        return slice((hh // 2) * HEAD_PAIR, (hh // 2 + 1) * HEAD_PAIR)

    qh, ct = [], []
    for hh in heads:
        q2 = q_ref[0, :, pair_lanes(hh)]
        qh.append(jnp.where(low_half if hh % 2 == 0 else ~low_half, q2, jnp.zeros_like(q2)))
        head = 2 * FOX_PAIRS * group + hh
        ct.append(jnp.sum(jnp.where(lane == head, ct_all, 0.0), axis=-1, keepdims=True))

    s = [[None] * n_tiles for _ in heads]
    m_rows = [None for _ in heads]
    shift = [None for _ in heads]
    pv = [None for _ in heads]

    def pass1(hh, t):
        keys = slice(t * tk, (t + 1) * tk)
        k_t = k_ref[0, keys, pair_lanes(hh)]
        s_t = _dot_nt(qh[hh], k_t) - cr_ref[0, pl.ds(2 * FOX_PAIRS * group + hh, 1), keys]
        if (t + 1) * tk > n_keys - tq:
            s_t = jnp.where(q_pos >= k_pos + t * tk, s_t, -jnp.inf)
        s[hh][t] = s_t
        m_t = jnp.max(s_t, axis=-1, keepdims=True)
        m_rows[hh] = m_t if t == 0 else jnp.maximum(m_rows[hh], m_t)

    def pass2(hh, t):
        if t == 0:
            shift[hh] = ct[hh] - (m_rows[hh] + ct[hh])
        v_t = v_ref[0, t * tk:(t + 1) * tk, pair_lanes(hh)]
        p_t = jnp.exp(s[hh][t] + shift[hh]).astype(BF16)
        v_h = jnp.where(v_low if hh % 2 == 0 else ~v_low, v_t, jnp.ones_like(v_t))
        pv_t = _dot(p_t, v_h)
        pv[hh] = pv_t if t == 0 else pv[hh] + pv_t

    for pp in range(FOX_PAIRS + 1):
        for t in range(n_tiles):
            for hh in (2 * pp, 2 * pp + 1):
                if pp < FOX_PAIRS:
                    pass1(hh, t)
            for hh in (2 * pp - 2, 2 * pp - 1):
                if pp >= 1:
                    pass2(hh, t)
    outs = [pv[hh] / pltpu.roll(pv[hh], HEAD_DIM, 1) for hh in heads]
    for pp in range(FOX_PAIRS):
        o_ref[0, :, pp * HEAD_PAIR:(pp + 1) * HEAD_PAIR] = jnp.where(
            low_half, outs[2 * pp], outs[2 * pp + 1]).astype(BF16)


def _fox_kernel(q_ref, k_ref, v_ref, ct_ref, cr_ref, o_ref):
    tq, seq = q_ref.shape[1], k_ref.shape[1]
    group = pl.program_id(1)
    qi = pl.program_id(2)
    for i in range(seq // tq):
        pl.when(qi == i)(functools.partial(
            _fox_tile, (i + 1) * tq, group, q_ref, k_ref, v_ref, ct_ref, cr_ref, o_ref))


def _fox_attention(q, k, v, c_tok, c_rows, tq):
    batch, seq, _ = q.shape
    width = FOX_PAIRS * HEAD_PAIR
    qspec = pl.BlockSpec((1, tq, width), lambda b, p, i: (b, i, p))
    kvspec = pl.BlockSpec((1, seq, width), lambda b, p, i: (b, 0, p))
    return pl.pallas_call(
        _fox_kernel,
        grid=(batch, TOK_WIDTH // width, seq // tq),
        in_specs=[
            qspec, kvspec, kvspec,
            pl.BlockSpec((1, tq, LANES), lambda b, p, i: (b, i, 0)),
            pl.BlockSpec((1, 2 * SUBLANES, seq), lambda b, p, i: (b, 0, 0)),
        ],
        out_specs=qspec,
        out_shape=jax.ShapeDtypeStruct((batch, seq, TOK_WIDTH), BF16),
        compiler_params=_params(3),
        name="fox_attention",
    )(q, k, v, c_tok, c_rows)


def _mixer_b_out_kernel(x_ref, tok_ref, qm_ref, kbd_ref, vbd_ref, wout_ref, o_ref):
    tm = x_ref.shape[1]
    blocks = [slice(r, r + MEM_ROWS) for r in range(0, tm, MEM_ROWS)]
    mem_o = _mem_attention([qm_ref[0, rows, :] for rows in blocks], kbd_ref, vbd_ref)
    mem_o = jnp.concatenate(mem_o, axis=0).astype(BF16)
    cat = jnp.concatenate([tok_ref[0], mem_o], axis=-1)
    o_ref[0] = x_ref[0] + _dot(cat, wout_ref[...])


def _mixer_b_out(x, tok, q_mem, kvbd, layer, w_out, tm):
    batch, seq, _ = x.shape

    def tile(width):
        return pl.BlockSpec((1, tm, width), lambda b, s: (b, s, 0))

    kvspec = pl.BlockSpec((1, 1, N_MEM_HEADS * N_MEM, MEM_WIDTH), lambda b, s: (layer, b, 0, 0))
    return pl.pallas_call(
        _mixer_b_out_kernel,
        grid=(batch, seq // tm),
        in_specs=[tile(D_MODEL), tile(TOK_WIDTH), tile(MEM_WIDTH), kvspec, kvspec,
                  _const_spec(w_out.shape)],
        out_specs=tile(D_MODEL),
        out_shape=jax.ShapeDtypeStruct(x.shape, F32),
        compiler_params=_params(2),
        name="mixer_b_out",
    )(x, tok, q_mem, kvbd[0], kvbd[1], w_out)


def kernel(x, mem, a_norm1, a_w_in, a_v_norm, a_w_s, a_b_s, a_mem_norm, a_w_mem_kv, a_w_out, a_norm2, a_ffn_in, a_ffn_conv, a_ffn_conv_b, a_ffn_out, kv_norm, w_kv, b_f, b_norm1, b_w_q, b_mem_norm, b_w_mem_kv, b_w_out, b_norm2, b_ffn_in, b_ffn_conv, b_ffn_conv_b, b_ffn_out, final_norm):
    n_a, n_b = a_norm1.shape[0], b_norm1.shape[0]
    assert n_a == 1 and n_b == 1, "one gMLP layer, then one forgetting-attention layer"
    batch, seq, _ = x.shape
    row = lambda a: a.reshape(1, -1)
    bf = lambda a: a.astype(BF16)

    kvbd = _memkv(mem,
                  jnp.concatenate([a_mem_norm, b_mem_norm], axis=0)[:, None, :],
                  bf(jnp.concatenate([a_w_mem_kv, b_w_mem_kv], axis=0)))

    b_full = jnp.repeat(a_b_s[0].T, TOK_WIDTH // GMLP_GROUPS, axis=1)
    x = _mixer_a(x, row(a_norm1[0]), bf(a_w_in[0]), row(a_v_norm[0]), a_w_s[0], b_full,
                 kvbd, 0, bf(a_w_out[0]), MIX_A_TM)
    x = _ffn(x, row(a_norm2[0]), bf(a_ffn_in[0]), a_ffn_conv[0], row(a_ffn_conv_b[0]),
             bf(a_ffn_out[0]), row(final_norm), FFN_TM, False)

    w_f = jnp.pad(w_kv[:, 2 * TOK_WIDTH:], ((0, 0), (0, LANES - N_FOX_HEADS)))
    b_f_pad = jnp.pad(b_f, (0, LANES - N_FOX_HEADS)).reshape(1, LANES)

    for j in range(n_b):
        k_sh, v_sh, c_tok, c_head, q_fox, q_mem = _proj_b(
            x, row(kv_norm), bf(w_kv[:, :2 * TOK_WIDTH]), bf(w_f), b_f_pad,
            row(b_norm1[j]), bf(b_w_q[j]), PROJ_TM)
        tok = _fox_attention(q_fox, k_sh, v_sh, c_tok, c_head, FOX_TQ)
        x = _mixer_b_out(x, tok, q_mem, kvbd, n_a + j, bf(b_w_out[j]), MIX_B_TM)
        x = _ffn(x, row(b_norm2[j]), bf(b_ffn_in[j]), b_ffn_conv[j], row(b_ffn_conv_b[j]),
                 bf(b_ffn_out[j]), row(final_norm), FFN_TM, j == n_b - 1)
    return x
```

```python
import functools

import jax
import jax.numpy as jnp
from jax import lax
from jax.experimental import pallas as pl
from jax.experimental.pallas import tpu as pltpu

D_MODEL = 1024
N_MEM = 256
HEAD_DIM = 64
N_MEM_HEADS = 4
MEM_WIDTH = N_MEM_HEADS * HEAD_DIM
TOK_WIDTH = D_MODEL - MEM_WIDTH
N_FOX_HEADS = TOK_WIDTH // HEAD_DIM
GMLP_BLOCK = 128
GMLP_GROUPS = 4
D_FF = 2816
CONV_WIDTH = 3
EPS = 1e-6
Q_SCALE = HEAD_DIM ** -0.5

LANES = 128
SUBLANES = 8
HEAD_PAIR = 2 * HEAD_DIM
FF_CHUNK = 256
FOX_TQ = 512
FOX_PAIRS = 3
FOX_TK = 512
FFN_TM = 512
CUMSUM_ROWS = 256
PROJ_TM = 1024
MIX_A_TM = 1024
IN_CHUNK = 256
MIX_B_TM = 1024
MEMKV_B = 4
MEM_ROWS = 256
VMEM_LIMIT = 56 * 1024 * 1024

BF16 = jnp.bfloat16
F32 = jnp.float32


def _const_spec(shape):
    nd = len(shape)
    return pl.BlockSpec(shape, lambda *_: (0,) * nd, pipeline_mode=pl.Buffered(1))


def _params(n_axes):
    return pltpu.CompilerParams(
        dimension_semantics=("arbitrary",) * n_axes, vmem_limit_bytes=VMEM_LIMIT)


def _rms(x, g):
    r = lax.rsqrt(jnp.mean(x * x, axis=-1, keepdims=True) + EPS)
    return x * r * g


def _dot(a, b):
    return jnp.dot(a, b, preferred_element_type=F32)


def _dot_nt(a, b):
    return lax.dot_general(a, b, (((1,), (1,)), ((), ())), preferred_element_type=F32)


def _memkv_kernel(mem_ref, g_ref, w_ref, kbd_ref, vbd_ref):
    n_layers = w_ref.shape[0]
    feat_head = lax.broadcasted_iota(jnp.int32, (N_MEM, MEM_WIDTH), 1) // HEAD_DIM
    for b in range(mem_ref.shape[0]):
        m = mem_ref[b]
        mr = m * lax.rsqrt(jnp.mean(m * m, axis=-1, keepdims=True) + EPS)
        for l in range(n_layers):
            kv = _dot((mr * g_ref[l]).astype(BF16), w_ref[l])
            k, v = kv[:, :MEM_WIDTH], kv[:, MEM_WIDTH:]
            for h in range(N_MEM_HEADS):
                rows = slice(h * N_MEM, (h + 1) * N_MEM)
                kbd_ref[l, b, rows, :] = jnp.where(feat_head == h, k, 0.0).astype(BF16)
                vbd_ref[l, b, rows, :] = jnp.where(feat_head == h, v, 0.0).astype(BF16)


def _memkv(mem, norms, ws):
    n_layers, batch = norms.shape[0], mem.shape[0]
    out = jax.ShapeDtypeStruct((n_layers, batch, N_MEM_HEADS * N_MEM, MEM_WIDTH), BF16)
    out_spec = pl.BlockSpec((n_layers, MEMKV_B, N_MEM_HEADS * N_MEM, MEM_WIDTH),
                            lambda b: (0, b, 0, 0))
    return pl.pallas_call(
        _memkv_kernel,
        grid=(batch // MEMKV_B,),
        in_specs=[
            pl.BlockSpec((MEMKV_B, N_MEM, D_MODEL), lambda b: (b, 0, 0)),
            _const_spec(norms.shape),
            _const_spec(ws.shape),
        ],
        out_specs=[out_spec, out_spec],
        out_shape=[out, out],
        compiler_params=_params(1),
        name="memkv",
    )(mem, norms, ws)


def _mem_softmax(logits):
    probs = []
    for h in range(N_MEM_HEADS):
        seg = logits[:, h * N_MEM:(h + 1) * N_MEM]
        e = jnp.exp(seg - jnp.max(seg, axis=-1, keepdims=True))
        probs.append((e / jnp.sum(e, axis=-1, keepdims=True)).astype(BF16))
    return jnp.concatenate(probs, axis=-1)


def _mem_attention(qs, kbd_ref, vbd_ref):
    logits = [_dot_nt(q, kbd_ref[0, 0]) for q in qs]
    probs = [_mem_softmax(l) for l in logits]
    return [_dot(p, vbd_ref[0, 0]) for p in probs]


def _gmlp_mix(vn, ws_ref):
    tm = vn.shape[0]
    tri = (lax.broadcasted_iota(jnp.int32, (GMLP_BLOCK, GMLP_BLOCK), 0)
           >= lax.broadcasted_iota(jnp.int32, (GMLP_BLOCK, GMLP_BLOCK), 1))
    w = [jnp.where(tri, ws_ref[g], 0.0).astype(BF16) for g in range(GMLP_GROUPS)]
    first_half = lax.broadcasted_iota(jnp.int32, (GMLP_BLOCK, LANES), 1) < HEAD_DIM
    win_lo = (0, 128, 384, 512)
    blocks = []
    for r in range(tm // GMLP_BLOCK):
        rows = slice(r * GMLP_BLOCK, (r + 1) * GMLP_BLOCK)
        m = [_dot(w[g], vn[rows, win_lo[g]:win_lo[g] + 2 * LANES]) for g in range(GMLP_GROUPS)]
        blocks.append(jnp.concatenate([
            m[0][:, :LANES],
            jnp.where(first_half, m[0][:, LANES:], m[1][:, :LANES]),
            m[1][:, LANES:],
            m[2][:, :LANES],
            jnp.where(first_half, m[2][:, LANES:], m[3][:, :LANES]),
            m[3][:, LANES:],
        ], axis=-1))
    return jnp.concatenate(blocks, axis=0)


def _cast_slices(weights, n_steps):
    views = [w.reshape(n_steps, -1, LANES) for w in weights]
    shapes = [jax.ShapeDtypeStruct(v.shape, BF16) for v in views]
    return views, shapes


def _cast_spec(view, step_of):
    return pl.BlockSpec((1,) + view.shape[1:], lambda *idx: (step_of(*idx), 0, 0))


def _mixer_a_kernel(x_ref, g1_ref, win_ref, vg_ref, ws_ref, bs_ref, kbd_ref, vbd_ref, wout_ref,
                    fi_ref, fo_ref, o_ref, fi_bf_ref, fo_bf_ref):
    fi_bf_ref[...] = fi_ref[...].astype(BF16)
    fo_bf_ref[...] = fo_ref[...].astype(BF16)
    x = x_ref[0]
    tm = x.shape[0]
    h = _rms(x, g1_ref[...]).astype(BF16)
    n_tok = TOK_WIDTH // IN_CHUNK

    def z_chunk(c):
        return _dot(h, win_ref[:, c * IN_CHUNK:(c + 1) * IN_CHUNK])

    zv = [z_chunk(n_tok)]
    v = []
    for c in range(1, n_tok):
        zv.append(z_chunk(n_tok + c))
        v.append(jax.nn.gelu(zv[c - 1]))
    zq = z_chunk(2 * n_tok)
    v.append(jax.nn.gelu(zv[n_tok - 1]))
    zu = [z_chunk(0)]
    ssq = sum(jnp.sum(vc * vc, axis=-1, keepdims=True) for vc in v)
    r = lax.rsqrt(ssq * (1.0 / TOK_WIDTH) + EPS)
    vn = jnp.concatenate(
        [(vc * r * vg_ref[:, c * IN_CHUNK:(c + 1) * IN_CHUNK]).astype(BF16)
         for c, vc in enumerate(v)], axis=-1)
    q_mem = (zq * Q_SCALE).astype(BF16)
    blocks = [slice(r0, r0 + MEM_ROWS) for r0 in range(0, tm, MEM_ROWS)]
    logits = [_dot_nt(q_mem[rows], kbd_ref[0, 0]) for rows in blocks]
    u, probs = [], []
    for c in range(1, n_tok):
        zu.append(z_chunk(c))
        u.append(jax.nn.gelu(zu[c - 1]))
        if c - 1 < len(logits):
            probs.append(_mem_softmax(logits[c - 1]))
    probs.extend(_mem_softmax(l) for l in logits[len(probs):])
    mem_o = [_dot(p, vbd_ref[0, 0]) for p in probs]
    u.append(jax.nn.gelu(zu[n_tok - 1]))
    bias = jnp.concatenate([bs_ref[...]] * (tm // GMLP_BLOCK), axis=0)
    tok = (jnp.concatenate(u, axis=-1) * (_gmlp_mix(vn, ws_ref) + bias)).astype(BF16)
    cat = jnp.concatenate([tok, jnp.concatenate(mem_o, axis=0).astype(BF16)], axis=-1)
    o_ref[0] = x + _dot(cat, wout_ref[...])


def _mixer_a(x, g1, w_in, v_g, w_s, b_full, kvbd, layer, w_out, ffn_weights, tm):
    batch, seq, _ = x.shape
    n_s = seq // tm
    xspec = pl.BlockSpec((1, tm, D_MODEL), lambda b, s: (b, s, 0))
    kvspec = pl.BlockSpec((1, 1, N_MEM_HEADS * N_MEM, MEM_WIDTH), lambda b, s: (layer, b, 0, 0))
    views, cast_shapes = _cast_slices(ffn_weights, batch * n_s)
    cast_specs = [_cast_spec(v, lambda b, s: b * n_s + s) for v in views]
    out, *casts = pl.pallas_call(
        _mixer_a_kernel,
        grid=(batch, n_s),
        in_specs=[
            xspec,
            _const_spec((1, D_MODEL)),
            _const_spec(w_in.shape),
            _const_spec((1, TOK_WIDTH)),
            _const_spec(w_s.shape),
            _const_spec(b_full.shape),
            kvspec, kvspec,
            _const_spec(w_out.shape),
        ] + cast_specs,
        out_specs=[xspec] + cast_specs,
        out_shape=[jax.ShapeDtypeStruct(x.shape, F32)] + cast_shapes,
        compiler_params=_params(2),
        name="mixer_a",
    )(x, g1, w_in, v_g, w_s, b_full, kvbd[0], kvbd[1], w_out, *views)
    return out, [c.reshape(w.shape) for c, w in zip(casts, ffn_weights)]


def _shift_rows(h, prev, n):
    rolled = pltpu.roll(h, n, 0)
    head = jnp.where(lax.broadcasted_iota(jnp.int32, prev.shape, 0) < n,
                     pltpu.roll(prev, n, 0), rolled[:SUBLANES])
    return jnp.concatenate([head, rolled[SUBLANES:]], axis=0)


def _ffn_kernel(x_ref, g_ref, win_ref, cw_ref, cb_ref, wout_ref, fn_ref, o_ref, carry_ref, act_ref,
                *, final_norm):
    @pl.when(pl.program_id(1) == 0)
    def _():
        carry_ref[...] = jnp.zeros_like(carry_ref)

    x = x_ref[0]
    tm = x.shape[0]
    hn = _rms(x, g_ref[...]).astype(BF16)

    def up_proj(col0):
        return _dot(hn, win_ref[:, col0:col0 + FF_CHUNK])

    def conv_cols(h, col0):
        cols = slice(col0, col0 + FF_CHUNK)
        prev = carry_ref[:, cols]
        carry_ref[:, cols] = h[tm - SUBLANES:]
        w = cw_ref[:, cols]
        return (cb_ref[:, cols] + w[2:3] * h
                + w[0:1] * _shift_rows(h, prev, 2) + w[1:2] * _shift_rows(h, prev, 1))

    for c in range(D_FF // FF_CHUNK):
        gate = conv_cols(up_proj(c * FF_CHUNK), c * FF_CHUNK)
        up = conv_cols(up_proj(D_FF + c * FF_CHUNK), D_FF + c * FF_CHUNK)
        half_gate = 0.5 * gate
        act_ref[:, c * FF_CHUNK:(c + 1) * FF_CHUNK] = (
            (half_gate + half_gate * jnp.tanh(half_gate)) * up).astype(BF16)
    y = x + _dot(act_ref[...], wout_ref[...])
    if final_norm:
        y = _rms(y, fn_ref[...])
    o_ref[0] = y


def _ffn(x, g, w_in, conv_w, conv_b, w_out, fn, tm, final_norm):
    batch, seq, _ = x.shape
    xspec = pl.BlockSpec((1, tm, D_MODEL), lambda b, s: (b, s, 0))
    return pl.pallas_call(
        functools.partial(_ffn_kernel, final_norm=final_norm),
        grid=(batch, seq // tm),
        in_specs=[
            xspec,
            _const_spec((1, D_MODEL)),
            _const_spec(w_in.shape),
            _const_spec(conv_w.shape),
            _const_spec(conv_b.shape),
            _const_spec(w_out.shape),
            _const_spec((1, D_MODEL)),
        ],
        out_specs=xspec,
        out_shape=jax.ShapeDtypeStruct(x.shape, F32),
        scratch_shapes=[pltpu.VMEM((SUBLANES, 2 * D_FF), F32), pltpu.VMEM((tm, D_FF), BF16)],
        compiler_params=_params(2),
        name="ffn_final" if final_norm else "ffn",
    )(x, g, w_in, conv_w, conv_b, w_out, fn)


def _split3(x):
    hi = x.astype(BF16)
    r = x - hi.astype(F32)
    mid = r.astype(BF16)
    lo = (r - mid.astype(F32)).astype(BF16)
    return hi, mid, lo


def _proj_b_kernel(x_ref, gkv_ref, wkv_ref, wf_ref, bf_ref, gq_ref, wq_ref,
                   k_ref, v_ref, ct_ref, cr_ref, qf_ref, qm_ref, carry_ref):
    @pl.when(pl.program_id(1) == 0)
    def _():
        carry_ref[...] = jnp.zeros_like(carry_ref)

    x = x_ref[0]
    tm = x.shape[0]
    xr = x * lax.rsqrt(jnp.mean(x * x, axis=-1, keepdims=True) + EPS)
    hkv = (xr * gkv_ref[...]).astype(BF16)
    hq = (xr * gq_ref[...]).astype(BF16)

    def project(h, w_ref, out_ref, c, col0, scale=None):
        z = _dot(h, w_ref[:, col0 + c * IN_CHUNK:col0 + (c + 1) * IN_CHUNK])
        if scale is not None:
            z = z * scale
        out_ref[0, :, c * IN_CHUNK:(c + 1) * IN_CHUNK] = z.astype(BF16)

    n_tok = TOK_WIDTH // IN_CHUNK
    f = _dot(hkv, wf_ref[...]) + bf_ref[...]
    project(hkv, wkv_ref, k_ref, 0, 0)
    log_f = jnp.minimum(f, 0.0) - jnp.log1p(jnp.exp(-jnp.abs(f)))
    tri = (lax.broadcasted_iota(jnp.int32, (CUMSUM_ROWS, CUMSUM_ROWS), 0)
           >= lax.broadcasted_iota(jnp.int32, (CUMSUM_ROWS, CUMSUM_ROWS), 1)).astype(BF16)
    parts = _split3(log_f)
    project(hkv, wkv_ref, k_ref, 1, 0)
    carry, c_blocks = carry_ref[0:1, :], []
    for r0 in range(0, tm, CUMSUM_ROWS):
        hi, mid, lo = (part[r0:r0 + CUMSUM_ROWS] for part in parts)
        c_blk = carry + ((_dot(tri, hi) + _dot(tri, mid)) + _dot(tri, lo))
        carry = c_blk[CUMSUM_ROWS - 1:CUMSUM_ROWS, :]
        c_blocks.append(c_blk)
    c = jnp.concatenate(c_blocks, axis=0)
    project(hkv, wkv_ref, k_ref, 2, 0)
    carry_ref[...] = jnp.broadcast_to(carry, carry_ref.shape)
    ct_ref[0] = c
    cr_ref[0] = c.T[:2 * SUBLANES, :]
    for j in range(n_tok):
        project(hkv, wkv_ref, v_ref, j, TOK_WIDTH)
    for j in range(n_tok):
        project(hq, wq_ref, qf_ref, j, 0, Q_SCALE)
    project(hq, wq_ref, qm_ref, 0, TOK_WIDTH, Q_SCALE)


def _proj_b(x, g_kv, w_kv, w_f, b_f, g_q, w_q, tm):
    batch, seq, _ = x.shape

    def tile(width):
        return pl.BlockSpec((1, tm, width), lambda b, s: (b, s, 0))

    return pl.pallas_call(
        _proj_b_kernel,
        grid=(batch, seq // tm),
        in_specs=[
            tile(D_MODEL),
            _const_spec((1, D_MODEL)),
            _const_spec(w_kv.shape),
            _const_spec(w_f.shape),
            _const_spec((1, LANES)),
            _const_spec((1, D_MODEL)),
            _const_spec(w_q.shape),
        ],
        out_specs=[
            tile(TOK_WIDTH), tile(TOK_WIDTH), tile(LANES),
            pl.BlockSpec((1, 2 * SUBLANES, tm), lambda b, s: (b, 0, s)),
            tile(TOK_WIDTH), tile(MEM_WIDTH),
        ],
        out_shape=[
            jax.ShapeDtypeStruct((batch, seq, TOK_WIDTH), BF16),
            jax.ShapeDtypeStruct((batch, seq, TOK_WIDTH), BF16),
            jax.ShapeDtypeStruct((batch, seq, LANES), F32),
            jax.ShapeDtypeStruct((batch, 2 * SUBLANES, seq), F32),
            jax.ShapeDtypeStruct((batch, seq, TOK_WIDTH), BF16),
            jax.ShapeDtypeStruct((batch, seq, MEM_WIDTH), BF16),
        ],
        scratch_shapes=[pltpu.VMEM((SUBLANES, LANES), F32)],
        compiler_params=_params(2),
        name="proj_b",
    )(x, g_kv, w_kv, w_f, b_f, g_q, w_q)


def _fox_tile(n_keys, group, q_ref, k_ref, v_ref, ct_ref, cr_ref, o_ref):
    tq, tk = q_ref.shape[1], FOX_TK
    n_tiles = n_keys // tk
    ct_all = ct_ref[0]
    lane = lax.broadcasted_iota(jnp.int32, (tq, LANES), 1)
    low_half = lane < HEAD_DIM
    v_low = lax.broadcasted_iota(jnp.int32, (tk, LANES), 1) < HEAD_DIM
    q_pos = lax.broadcasted_iota(jnp.int32, (tq, tk), 0) + (n_keys - tq)
    k_pos = lax.broadcasted_iota(jnp.int32, (tq, tk), 1)
    heads = range(2 * FOX_PAIRS)

    def pair_lanes(hh):
        return slice((hh // 2) * HEAD_PAIR, (hh // 2 + 1) * HEAD_PAIR)

    qh, ct = [], []
    for hh in heads:
        q2 = q_ref[0, :, pair_lanes(hh)]
        qh.append(jnp.where(low_half if hh % 2 == 0 else ~low_half, q2, jnp.zeros_like(q2)))
        head = 2 * FOX_PAIRS * group + hh
        ct.append(jnp.sum(jnp.where(lane == head, ct_all, 0.0), axis=-1, keepdims=True))

    s = [[None] * n_tiles for _ in heads]
    m_rows = [None for _ in heads]
    shift = [None for _ in heads]
    pv = [None for _ in heads]

    def pass1(hh, t):
        keys = slice(t * tk, (t + 1) * tk)
        k_t = k_ref[0, keys, pair_lanes(hh)]
        s_t = _dot_nt(qh[hh], k_t) - cr_ref[0, pl.ds(2 * FOX_PAIRS * group + hh, 1), keys]
        if (t + 1) * tk > n_keys - tq:
            s_t = jnp.where(q_pos >= k_pos + t * tk, s_t, -jnp.inf)
        s[hh][t] = s_t
        m_t = jnp.max(s_t, axis=-1, keepdims=True)
        m_rows[hh] = m_t if t == 0 else jnp.maximum(m_rows[hh], m_t)

    def pass2(hh, t):
        if t == 0:
            shift[hh] = ct[hh] - (m_rows[hh] + ct[hh])
        v_t = v_ref[0, t * tk:(t + 1) * tk, pair_lanes(hh)]
        p_t = jnp.exp(s[hh][t] + shift[hh]).astype(BF16)
        v_h = jnp.where(v_low if hh % 2 == 0 else ~v_low, v_t, jnp.ones_like(v_t))
        pv_t = _dot(p_t, v_h)
        pv[hh] = pv_t if t == 0 else pv[hh] + pv_t

    for pp in range(FOX_PAIRS + 1):
        for t in range(n_tiles):
            for hh in (2 * pp, 2 * pp + 1):
                if pp < FOX_PAIRS:
                    pass1(hh, t)
            for hh in (2 * pp - 2, 2 * pp - 1):
                if pp >= 1:
                    pass2(hh, t)
    outs = [pv[hh] / pltpu.roll(pv[hh], HEAD_DIM, 1) for hh in heads]
    for pp in range(FOX_PAIRS):
        o_ref[0, :, pp * HEAD_PAIR:(pp + 1) * HEAD_PAIR] = jnp.where(
            low_half, outs[2 * pp], outs[2 * pp + 1]).astype(BF16)


def _fox_kernel(q_ref, k_ref, v_ref, ct_ref, cr_ref, fi_ref, fo_ref, o_ref, fi_bf_ref, fo_bf_ref):
    fi_bf_ref[...] = fi_ref[...].astype(BF16)
    fo_bf_ref[...] = fo_ref[...].astype(BF16)
    tq, seq = q_ref.shape[1], k_ref.shape[1]
    group = pl.program_id(1)
    qi = pl.program_id(2)
    for i in range(seq // tq):
        pl.when(qi == i)(functools.partial(
            _fox_tile, (i + 1) * tq, group, q_ref, k_ref, v_ref, ct_ref, cr_ref, o_ref))


def _fox_attention(q, k, v, c_tok, c_rows, ffn_weights, tq):
    batch, seq, _ = q.shape
    width = FOX_PAIRS * HEAD_PAIR
    n_p, n_i = TOK_WIDTH // width, seq // tq
    qspec = pl.BlockSpec((1, tq, width), lambda b, p, i: (b, i, p))
    kvspec = pl.BlockSpec((1, seq, width), lambda b, p, i: (b, 0, p))
    views, cast_shapes = _cast_slices(ffn_weights, batch * n_p * n_i)
    cast_specs = [_cast_spec(v, lambda b, p, i: (b * n_p + p) * n_i + i) for v in views]
    out, *casts = pl.pallas_call(
        _fox_kernel,
        grid=(batch, n_p, n_i),
        in_specs=[
            qspec, kvspec, kvspec,
            pl.BlockSpec((1, tq, LANES), lambda b, p, i: (b, i, 0)),
            pl.BlockSpec((1, 2 * SUBLANES, seq), lambda b, p, i: (b, 0, 0)),
        ] + cast_specs,
        out_specs=[qspec] + cast_specs,
        out_shape=[jax.ShapeDtypeStruct((batch, seq, TOK_WIDTH), BF16)] + cast_shapes,
        compiler_params=_params(3),
        name="fox_attention",
    )(q, k, v, c_tok, c_rows, *views)
    return out, [c.reshape(w.shape) for c, w in zip(casts, ffn_weights)]


def _mixer_b_out_kernel(x_ref, tok_ref, qm_ref, kbd_ref, vbd_ref, wout_ref, o_ref):
    tm = x_ref.shape[1]
    blocks = [slice(r, r + MEM_ROWS) for r in range(0, tm, MEM_ROWS)]
    mem_o = _mem_attention([qm_ref[0, rows, :] for rows in blocks], kbd_ref, vbd_ref)
    mem_o = jnp.concatenate(mem_o, axis=0).astype(BF16)
    cat = jnp.concatenate([tok_ref[0], mem_o], axis=-1)
    o_ref[0] = x_ref[0] + _dot(cat, wout_ref[...])


def _mixer_b_out(x, tok, q_mem, kvbd, layer, w_out, tm):
    batch, seq, _ = x.shape

    def tile(width):
        return pl.BlockSpec((1, tm, width), lambda b, s: (b, s, 0))

    kvspec = pl.BlockSpec((1, 1, N_MEM_HEADS * N_MEM, MEM_WIDTH), lambda b, s: (layer, b, 0, 0))
    return pl.pallas_call(
        _mixer_b_out_kernel,
        grid=(batch, seq // tm),
        in_specs=[tile(D_MODEL), tile(TOK_WIDTH), tile(MEM_WIDTH), kvspec, kvspec,
                  _const_spec(w_out.shape)],
        out_specs=tile(D_MODEL),
        out_shape=jax.ShapeDtypeStruct(x.shape, F32),
        compiler_params=_params(2),
        name="mixer_b_out",
    )(x, tok, q_mem, kvbd[0], kvbd[1], w_out)


def kernel(x, mem, a_norm1, a_w_in, a_v_norm, a_w_s, a_b_s, a_mem_norm, a_w_mem_kv, a_w_out, a_norm2, a_ffn_in, a_ffn_conv, a_ffn_conv_b, a_ffn_out, kv_norm, w_kv, b_f, b_norm1, b_w_q, b_mem_norm, b_w_mem_kv, b_w_out, b_norm2, b_ffn_in, b_ffn_conv, b_ffn_conv_b, b_ffn_out, final_norm):
    n_a, n_b = a_norm1.shape[0], b_norm1.shape[0]
    assert n_a == 1 and n_b == 1, "one gMLP layer, then one forgetting-attention layer"
    batch, seq, _ = x.shape
    row = lambda a: a.reshape(1, -1)
    bf = lambda a: a.astype(BF16)

    kvbd = _memkv(mem,
                  jnp.concatenate([a_mem_norm, b_mem_norm], axis=0)[:, None, :],
                  bf(jnp.concatenate([a_w_mem_kv, b_w_mem_kv], axis=0)))

    b_full = jnp.repeat(a_b_s[0].T, TOK_WIDTH // GMLP_GROUPS, axis=1)
    x, (ffn_in, ffn_out) = _mixer_a(
        x, row(a_norm1[0]), bf(a_w_in[0]), row(a_v_norm[0]), a_w_s[0], b_full,
        kvbd, 0, bf(a_w_out[0]), (a_ffn_in[0], a_ffn_out[0]), MIX_A_TM)
    x = _ffn(x, row(a_norm2[0]), ffn_in, a_ffn_conv[0], row(a_ffn_conv_b[0]),
             ffn_out, row(final_norm), FFN_TM, False)

    w_f = jnp.pad(w_kv[:, 2 * TOK_WIDTH:], ((0, 0), (0, LANES - N_FOX_HEADS)))
    b_f_pad = jnp.pad(b_f, (0, LANES - N_FOX_HEADS)).reshape(1, LANES)

    for j in range(n_b):
        k_sh, v_sh, c_tok, c_head, q_fox, q_mem = _proj_b(
            x, row(kv_norm), bf(w_kv[:, :2 * TOK_WIDTH]), bf(w_f), b_f_pad,
            row(b_norm1[j]), bf(b_w_q[j]), PROJ_TM)
        tok, (ffn_in, ffn_out) = _fox_attention(
            q_fox, k_sh, v_sh, c_tok, c_head, (b_ffn_in[j], b_ffn_out[j]), FOX_TQ)
        x = _mixer_b_out(x, tok, q_mem, kvbd, n_a + j, bf(b_w_out[j]), MIX_B_TM)
        x = _ffn(x, row(b_norm2[j]), ffn_in, b_ffn_conv[j], row(b_ffn_conv_b[j]),
                 ffn_out, row(final_norm), FFN_TM, j == n_b - 1)
    return x
```

```python
import functools

import jax
import jax.numpy as jnp
from jax import lax
from jax.experimental import pallas as pl
from jax.experimental.pallas import tpu as pltpu

D_MODEL = 1024
N_MEM = 256
HEAD_DIM = 64
N_MEM_HEADS = 4
MEM_WIDTH = N_MEM_HEADS * HEAD_DIM
TOK_WIDTH = D_MODEL - MEM_WIDTH
N_FOX_HEADS = TOK_WIDTH // HEAD_DIM
GMLP_BLOCK = 128
GMLP_GROUPS = 4
D_FF = 2816
CONV_WIDTH = 3
EPS = 1e-6
Q_SCALE = HEAD_DIM ** -0.5

LANES = 128
SUBLANES = 8
HEAD_PAIR = 2 * HEAD_DIM
FF_CHUNK = 256
FOX_TQ = 512
FOX_PAIRS = 3
FOX_TK = 512
FFN_TM = 512
CUMSUM_ROWS = 256
PROJ_TM = 1024
MIX_A_TM = 1024
IN_CHUNK = 256
MIX_B_TM = 1024
CAST_BLOCKS = 16
MEMKV_B = 4
MEM_ROWS = 256
VMEM_LIMIT = 56 * 1024 * 1024

BF16 = jnp.bfloat16
F32 = jnp.float32


def _const_spec(shape):
    nd = len(shape)
    return pl.BlockSpec(shape, lambda *_: (0,) * nd, pipeline_mode=pl.Buffered(1))


def _params(n_axes):
    return pltpu.CompilerParams(
        dimension_semantics=("arbitrary",) * n_axes, vmem_limit_bytes=VMEM_LIMIT)


def _rms(x, g):
    r = lax.rsqrt(jnp.mean(x * x, axis=-1, keepdims=True) + EPS)
    return x * r * g


def _dot(a, b):
    return jnp.dot(a, b, preferred_element_type=F32)


def _dot_nt(a, b):
    return lax.dot_general(a, b, (((1,), (1,)), ((), ())), preferred_element_type=F32)


def _memkv_kernel(mem_ref, g_ref, w_ref, kbd_ref, vbd_ref):
    n_layers = w_ref.shape[0]
    feat_head = lax.broadcasted_iota(jnp.int32, (N_MEM, MEM_WIDTH), 1) // HEAD_DIM
    for b in range(mem_ref.shape[0]):
        m = mem_ref[b]
        mr = m * lax.rsqrt(jnp.mean(m * m, axis=-1, keepdims=True) + EPS)
        for l in range(n_layers):
            kv = _dot((mr * g_ref[l]).astype(BF16), w_ref[l])
            k, v = kv[:, :MEM_WIDTH], kv[:, MEM_WIDTH:]
            for h in range(N_MEM_HEADS):
                rows = slice(h * N_MEM, (h + 1) * N_MEM)
                kbd_ref[l, b, rows, :] = jnp.where(feat_head == h, k, 0.0).astype(BF16)
                vbd_ref[l, b, rows, :] = jnp.where(feat_head == h, v, 0.0).astype(BF16)


def _memkv(mem, norms, ws):
    n_layers, batch = norms.shape[0], mem.shape[0]
    out = jax.ShapeDtypeStruct((n_layers, batch, N_MEM_HEADS * N_MEM, MEM_WIDTH), BF16)
    out_spec = pl.BlockSpec((n_layers, MEMKV_B, N_MEM_HEADS * N_MEM, MEM_WIDTH),
                            lambda b: (0, b, 0, 0))
    return pl.pallas_call(
        _memkv_kernel,
        grid=(batch // MEMKV_B,),
        in_specs=[
            pl.BlockSpec((MEMKV_B, N_MEM, D_MODEL), lambda b: (b, 0, 0)),
            _const_spec(norms.shape),
            _const_spec(ws.shape),
        ],
        out_specs=[out_spec, out_spec],
        out_shape=[out, out],
        compiler_params=_params(1),
        name="memkv",
    )(mem, norms, ws)


def _mem_softmax(logits):
    probs = []
    for h in range(N_MEM_HEADS):
        seg = logits[:, h * N_MEM:(h + 1) * N_MEM]
        e = jnp.exp(seg - jnp.max(seg, axis=-1, keepdims=True))
        probs.append((e / jnp.sum(e, axis=-1, keepdims=True)).astype(BF16))
    return jnp.concatenate(probs, axis=-1)


def _mem_attention(qs, kbd_ref, vbd_ref):
    logits = [_dot_nt(q, kbd_ref[0, 0]) for q in qs]
    probs = [_mem_softmax(l) for l in logits]
    return [_dot(p, vbd_ref[0, 0]) for p in probs]


def _gmlp_mix(vn, ws_ref):
    tm = vn.shape[0]
    tri = (lax.broadcasted_iota(jnp.int32, (GMLP_BLOCK, GMLP_BLOCK), 0)
           >= lax.broadcasted_iota(jnp.int32, (GMLP_BLOCK, GMLP_BLOCK), 1))
    w = [jnp.where(tri, ws_ref[g], 0.0).astype(BF16) for g in range(GMLP_GROUPS)]
    first_half = lax.broadcasted_iota(jnp.int32, (GMLP_BLOCK, LANES), 1) < HEAD_DIM
    win_lo = (0, 128, 384, 512)
    blocks = []
    for r in range(tm // GMLP_BLOCK):
        rows = slice(r * GMLP_BLOCK, (r + 1) * GMLP_BLOCK)
        m = [_dot(w[g], vn[rows, win_lo[g]:win_lo[g] + 2 * LANES]) for g in range(GMLP_GROUPS)]
        blocks.append(jnp.concatenate([
            m[0][:, :LANES],
            jnp.where(first_half, m[0][:, LANES:], m[1][:, :LANES]),
            m[1][:, LANES:],
            m[2][:, :LANES],
            jnp.where(first_half, m[2][:, LANES:], m[3][:, :LANES]),
            m[3][:, LANES:],
        ], axis=-1))
    return jnp.concatenate(blocks, axis=0)


def _cast_spec(w, step_of, n_steps):
    rows = w.shape[0] // CAST_BLOCKS
    return pl.BlockSpec((rows, w.shape[1]),
                        lambda *idx: (step_of(*idx) * CAST_BLOCKS // n_steps, 0))


def _mixer_a_kernel(x_ref, g1_ref, win_ref, vg_ref, ws_ref, bs_ref, kbd_ref, vbd_ref, wout_ref,
                    fi_ref, fo_ref, o_ref, fi_bf_ref, fo_bf_ref):
    fi_bf_ref[...] = fi_ref[...].astype(BF16)
    fo_bf_ref[...] = fo_ref[...].astype(BF16)
    x = x_ref[0]
    tm = x.shape[0]
    h = _rms(x, g1_ref[...]).astype(BF16)
    n_tok = TOK_WIDTH // IN_CHUNK

    def z_chunk(c):
        return _dot(h, win_ref[:, c * IN_CHUNK:(c + 1) * IN_CHUNK])

    zv = [z_chunk(n_tok)]
    v = []
    for c in range(1, n_tok):
        zv.append(z_chunk(n_tok + c))
        v.append(jax.nn.gelu(zv[c - 1]))
    zq = z_chunk(2 * n_tok)
    v.append(jax.nn.gelu(zv[n_tok - 1]))
    zu = [z_chunk(0)]
    ssq = sum(jnp.sum(vc * vc, axis=-1, keepdims=True) for vc in v)
    r = lax.rsqrt(ssq * (1.0 / TOK_WIDTH) + EPS)
    vn = jnp.concatenate(
        [(vc * r * vg_ref[:, c * IN_CHUNK:(c + 1) * IN_CHUNK]).astype(BF16)
         for c, vc in enumerate(v)], axis=-1)
    q_mem = (zq * Q_SCALE).astype(BF16)
    blocks = [slice(r0, r0 + MEM_ROWS) for r0 in range(0, tm, MEM_ROWS)]
    logits = [_dot_nt(q_mem[rows], kbd_ref[0, 0]) for rows in blocks]
    u, probs = [], []
    for c in range(1, n_tok):
        zu.append(z_chunk(c))
        u.append(jax.nn.gelu(zu[c - 1]))
        if c - 1 < len(logits):
            probs.append(_mem_softmax(logits[c - 1]))
    probs.extend(_mem_softmax(l) for l in logits[len(probs):])
    mem_o = [_dot(p, vbd_ref[0, 0]) for p in probs]
    u.append(jax.nn.gelu(zu[n_tok - 1]))
    bias = jnp.concatenate([bs_ref[...]] * (tm // GMLP_BLOCK), axis=0)
    tok = (jnp.concatenate(u, axis=-1) * (_gmlp_mix(vn, ws_ref) + bias)).astype(BF16)
    cat = jnp.concatenate([tok, jnp.concatenate(mem_o, axis=0).astype(BF16)], axis=-1)
    o_ref[0] = x + _dot(cat, wout_ref[...])


def _mixer_a(x, g1, w_in, v_g, w_s, b_full, kvbd, layer, w_out, ffn_weights, tm):
    batch, seq, _ = x.shape
    n_s = seq // tm
    xspec = pl.BlockSpec((1, tm, D_MODEL), lambda b, s: (b, s, 0))
    kvspec = pl.BlockSpec((1, 1, N_MEM_HEADS * N_MEM, MEM_WIDTH), lambda b, s: (layer, b, 0, 0))
    cast_specs = [_cast_spec(w, lambda b, s: b * n_s + s, batch * n_s) for w in ffn_weights]
    cast_shapes = [jax.ShapeDtypeStruct(w.shape, BF16) for w in ffn_weights]
    out, *casts = pl.pallas_call(
        _mixer_a_kernel,
        grid=(batch, n_s),
        in_specs=[
            xspec,
            _const_spec((1, D_MODEL)),
            _const_spec(w_in.shape),
            _const_spec((1, TOK_WIDTH)),
            _const_spec(w_s.shape),
            _const_spec(b_full.shape),
            kvspec, kvspec,
            _const_spec(w_out.shape),
        ] + cast_specs,
        out_specs=[xspec] + cast_specs,
        out_shape=[jax.ShapeDtypeStruct(x.shape, F32)] + cast_shapes,
        compiler_params=_params(2),
        name="mixer_a",
    )(x, g1, w_in, v_g, w_s, b_full, kvbd[0], kvbd[1], w_out, *ffn_weights)
    return out, casts


def _shift_rows(h, prev, n):
    rolled = pltpu.roll(h, n, 0)
    head = jnp.where(lax.broadcasted_iota(jnp.int32, prev.shape, 0) < n,
                     pltpu.roll(prev, n, 0), rolled[:SUBLANES])
    return jnp.concatenate([head, rolled[SUBLANES:]], axis=0)


def _ffn_kernel(x_ref, g_ref, win_ref, cw_ref, cb_ref, wout_ref, fn_ref, o_ref, carry_ref, act_ref,
                *, final_norm):
    @pl.when(pl.program_id(1) == 0)
    def _():
        carry_ref[...] = jnp.zeros_like(carry_ref)

    x = x_ref[0]
    tm = x.shape[0]
    hn = _rms(x, g_ref[...]).astype(BF16)

    def up_proj(col0):
        return _dot(hn, win_ref[:, col0:col0 + FF_CHUNK])

    def conv_cols(h, col0):
        cols = slice(col0, col0 + FF_CHUNK)
        prev = carry_ref[:, cols]
        carry_ref[:, cols] = h[tm - SUBLANES:]
        w = cw_ref[:, cols]
        return (cb_ref[:, cols] + w[2:3] * h
                + w[0:1] * _shift_rows(h, prev, 2) + w[1:2] * _shift_rows(h, prev, 1))

    for c in range(D_FF // FF_CHUNK):
        gate = conv_cols(up_proj(c * FF_CHUNK), c * FF_CHUNK)
        up = conv_cols(up_proj(D_FF + c * FF_CHUNK), D_FF + c * FF_CHUNK)
        half_gate = 0.5 * gate
        act_ref[:, c * FF_CHUNK:(c + 1) * FF_CHUNK] = (
            (half_gate + half_gate * jnp.tanh(half_gate)) * up).astype(BF16)
    y = x + _dot(act_ref[...], wout_ref[...])
    if final_norm:
        y = _rms(y, fn_ref[...])
    o_ref[0] = y


def _ffn(x, g, w_in, conv_w, conv_b, w_out, fn, tm, final_norm):
    batch, seq, _ = x.shape
    xspec = pl.BlockSpec((1, tm, D_MODEL), lambda b, s: (b, s, 0))
    return pl.pallas_call(
        functools.partial(_ffn_kernel, final_norm=final_norm),
        grid=(batch, seq // tm),
        in_specs=[
            xspec,
            _const_spec((1, D_MODEL)),
            _const_spec(w_in.shape),
            _const_spec(conv_w.shape),
            _const_spec(conv_b.shape),
            _const_spec(w_out.shape),
            _const_spec((1, D_MODEL)),
        ],
        out_specs=xspec,
        out_shape=jax.ShapeDtypeStruct(x.shape, F32),
        scratch_shapes=[pltpu.VMEM((SUBLANES, 2 * D_FF), F32), pltpu.VMEM((tm, D_FF), BF16)],
        compiler_params=_params(2),
        name="ffn_final" if final_norm else "ffn",
    )(x, g, w_in, conv_w, conv_b, w_out, fn)


def _split3(x):
    hi = x.astype(BF16)
    r = x - hi.astype(F32)
    mid = r.astype(BF16)
    lo = (r - mid.astype(F32)).astype(BF16)
    return hi, mid, lo


def _proj_b_kernel(x_ref, gkv_ref, wkv_ref, wf_ref, bf_ref, gq_ref, wq_ref,
                   k_ref, v_ref, ct_ref, cr_ref, qf_ref, qm_ref, carry_ref):
    @pl.when(pl.program_id(1) == 0)
    def _():
        carry_ref[...] = jnp.zeros_like(carry_ref)

    x = x_ref[0]
    tm = x.shape[0]
    xr = x * lax.rsqrt(jnp.mean(x * x, axis=-1, keepdims=True) + EPS)
    hkv = (xr * gkv_ref[...]).astype(BF16)
    hq = (xr * gq_ref[...]).astype(BF16)

    def project(h, w_ref, out_ref, c, col0, scale=None):
        z = _dot(h, w_ref[:, col0 + c * IN_CHUNK:col0 + (c + 1) * IN_CHUNK])
        if scale is not None:
            z = z * scale
        out_ref[0, :, c * IN_CHUNK:(c + 1) * IN_CHUNK] = z.astype(BF16)

    n_tok = TOK_WIDTH // IN_CHUNK
    f = _dot(hkv, wf_ref[...]) + bf_ref[...]
    project(hkv, wkv_ref, k_ref, 0, 0)
    log_f = jnp.minimum(f, 0.0) - jnp.log1p(jnp.exp(-jnp.abs(f)))
    tri = (lax.broadcasted_iota(jnp.int32, (CUMSUM_ROWS, CUMSUM_ROWS), 0)
           >= lax.broadcasted_iota(jnp.int32, (CUMSUM_ROWS, CUMSUM_ROWS), 1)).astype(BF16)
    parts = _split3(log_f)
    project(hkv, wkv_ref, k_ref, 1, 0)
    carry, c_blocks = carry_ref[0:1, :], []
    for r0 in range(0, tm, CUMSUM_ROWS):
        hi, mid, lo = (part[r0:r0 + CUMSUM_ROWS] for part in parts)
        c_blk = carry + ((_dot(tri, hi) + _dot(tri, mid)) + _dot(tri, lo))
        carry = c_blk[CUMSUM_ROWS - 1:CUMSUM_ROWS, :]
        c_blocks.append(c_blk)
    c = jnp.concatenate(c_blocks, axis=0)
    project(hkv, wkv_ref, k_ref, 2, 0)
    carry_ref[...] = jnp.broadcast_to(carry, carry_ref.shape)
    ct_ref[0] = c
    cr_ref[0] = c.T[:2 * SUBLANES, :]
    for j in range(n_tok):
        project(hkv, wkv_ref, v_ref, j, TOK_WIDTH)
    for j in range(n_tok):
        project(hq, wq_ref, qf_ref, j, 0, Q_SCALE)
    project(hq, wq_ref, qm_ref, 0, TOK_WIDTH, Q_SCALE)


def _proj_b(x, g_kv, w_kv, w_f, b_f, g_q, w_q, tm):
    batch, seq, _ = x.shape

    def tile(width):
        return pl.BlockSpec((1, tm, width), lambda b, s: (b, s, 0))

    return pl.pallas_call(
        _proj_b_kernel,
        grid=(batch, seq // tm),
        in_specs=[
            tile(D_MODEL),
            _const_spec((1, D_MODEL)),
            _const_spec(w_kv.shape),
            _const_spec(w_f.shape),
            _const_spec((1, LANES)),
            _const_spec((1, D_MODEL)),
            _const_spec(w_q.shape),
        ],
        out_specs=[
            tile(TOK_WIDTH), tile(TOK_WIDTH), tile(LANES),
            pl.BlockSpec((1, 2 * SUBLANES, tm), lambda b, s: (b, 0, s)),
            tile(TOK_WIDTH), tile(MEM_WIDTH),
        ],
        out_shape=[
            jax.ShapeDtypeStruct((batch, seq, TOK_WIDTH), BF16),
            jax.ShapeDtypeStruct((batch, seq, TOK_WIDTH), BF16),
            jax.ShapeDtypeStruct((batch, seq, LANES), F32),
            jax.ShapeDtypeStruct((batch, 2 * SUBLANES, seq), F32),
            jax.ShapeDtypeStruct((batch, seq, TOK_WIDTH), BF16),
            jax.ShapeDtypeStruct((batch, seq, MEM_WIDTH), BF16),
        ],
        scratch_shapes=[pltpu.VMEM((SUBLANES, LANES), F32)],
        compiler_params=_params(2),
        name="proj_b",
    )(x, g_kv, w_kv, w_f, b_f, g_q, w_q)


def _fox_tile(n_keys, group, q_ref, k_ref, v_ref, ct_ref, cr_ref, o_ref):
    tq, tk = q_ref.shape[1], FOX_TK
    n_tiles = n_keys // tk
    ct_all = ct_ref[0]
    lane = lax.broadcasted_iota(jnp.int32, (tq, LANES), 1)
    low_half = lane < HEAD_DIM
    v_low = lax.broadcasted_iota(jnp.int32, (tk, LANES), 1) < HEAD_DIM
    q_pos = lax.broadcasted_iota(jnp.int32, (tq, tk), 0) + (n_keys - tq)
    k_pos = lax.broadcasted_iota(jnp.int32, (tq, tk), 1)
    heads = range(2 * FOX_PAIRS)

    def pair_lanes(hh):
        return slice((hh // 2) * HEAD_PAIR, (hh // 2 + 1) * HEAD_PAIR)

    qh, ct = [], []
    for hh in heads:
        q2 = q_ref[0, :, pair_lanes(hh)]
        qh.append(jnp.where(low_half if hh % 2 == 0 else ~low_half, q2, jnp.zeros_like(q2)))
        head = 2 * FOX_PAIRS * group + hh
        ct.append(jnp.sum(jnp.where(lane == head, ct_all, 0.0), axis=-1, keepdims=True))

    s = [[None] * n_tiles for _ in heads]
    m_rows = [None for _ in heads]
    shift = [None for _ in heads]
    pv = [None for _ in heads]

    def pass1(hh, t):
        keys = slice(t * tk, (t + 1) * tk)
        k_t = k_ref[0, keys, pair_lanes(hh)]
        s_t = _dot_nt(qh[hh], k_t) - cr_ref[0, pl.ds(2 * FOX_PAIRS * group + hh, 1), keys]
        if (t + 1) * tk > n_keys - tq:
            s_t = jnp.where(q_pos >= k_pos + t * tk, s_t, -jnp.inf)
        s[hh][t] = s_t
        m_t = jnp.max(s_t, axis=-1, keepdims=True)
        m_rows[hh] = m_t if t == 0 else jnp.maximum(m_rows[hh], m_t)

    def pass2(hh, t):
        if t == 0:
            shift[hh] = ct[hh] - (m_rows[hh] + ct[hh])
        v_t = v_ref[0, t * tk:(t + 1) * tk, pair_lanes(hh)]
        p_t = jnp.exp(s[hh][t] + shift[hh]).astype(BF16)
        v_h = jnp.where(v_low if hh % 2 == 0 else ~v_low, v_t, jnp.ones_like(v_t))
        pv_t = _dot(p_t, v_h)
        pv[hh] = pv_t if t == 0 else pv[hh] + pv_t

    for pp in range(FOX_PAIRS + 1):
        for t in range(n_tiles):
            for hh in (2 * pp, 2 * pp + 1):
                if pp < FOX_PAIRS:
                    pass1(hh, t)
            for hh in (2 * pp - 2, 2 * pp - 1):
                if pp >= 1:
                    pass2(hh, t)
    outs = [pv[hh] / pltpu.roll(pv[hh], HEAD_DIM, 1) for hh in heads]
    for pp in range(FOX_PAIRS):
        o_ref[0, :, pp * HEAD_PAIR:(pp + 1) * HEAD_PAIR] = jnp.where(
            low_half, outs[2 * pp], outs[2 * pp + 1]).astype(BF16)


def _fox_kernel(q_ref, k_ref, v_ref, ct_ref, cr_ref, fi_ref, fo_ref, o_ref, fi_bf_ref, fo_bf_ref):
    fi_bf_ref[...] = fi_ref[...].astype(BF16)
    fo_bf_ref[...] = fo_ref[...].astype(BF16)
    tq, seq = q_ref.shape[1], k_ref.shape[1]
    group = pl.program_id(1)
    qi = pl.program_id(2)
    for i in range(seq // tq):
        pl.when(qi == i)(functools.partial(
            _fox_tile, (i + 1) * tq, group, q_ref, k_ref, v_ref, ct_ref, cr_ref, o_ref))


def _fox_attention(q, k, v, c_tok, c_rows, ffn_weights, tq):
    batch, seq, _ = q.shape
    width = FOX_PAIRS * HEAD_PAIR
    n_p, n_i = TOK_WIDTH // width, seq // tq
    qspec = pl.BlockSpec((1, tq, width), lambda b, p, i: (b, i, p))
    kvspec = pl.BlockSpec((1, seq, width), lambda b, p, i: (b, 0, p))
    cast_specs = [_cast_spec(w, lambda b, p, i: (b * n_p + p) * n_i + i, batch * n_p * n_i)
                  for w in ffn_weights]
    cast_shapes = [jax.ShapeDtypeStruct(w.shape, BF16) for w in ffn_weights]
    out, *casts = pl.pallas_call(
        _fox_kernel,
        grid=(batch, n_p, n_i),
        in_specs=[
            qspec, kvspec, kvspec,
            pl.BlockSpec((1, tq, LANES), lambda b, p, i: (b, i, 0)),
            pl.BlockSpec((1, 2 * SUBLANES, seq), lambda b, p, i: (b, 0, 0)),
        ] + cast_specs,
        out_specs=[qspec] + cast_specs,
        out_shape=[jax.ShapeDtypeStruct((batch, seq, TOK_WIDTH), BF16)] + cast_shapes,
        compiler_params=_params(3),
        name="fox_attention",
    )(q, k, v, c_tok, c_rows, *ffn_weights)
    return out, casts


def _mixer_b_out_kernel(x_ref, tok_ref, qm_ref, kbd_ref, vbd_ref, wout_ref, o_ref):
    tm = x_ref.shape[1]
    blocks = [slice(r, r + MEM_ROWS) for r in range(0, tm, MEM_ROWS)]
    mem_o = _mem_attention([qm_ref[0, rows, :] for rows in blocks], kbd_ref, vbd_ref)
    mem_o = jnp.concatenate(mem_o, axis=0).astype(BF16)
    cat = jnp.concatenate([tok_ref[0], mem_o], axis=-1)
    o_ref[0] = x_ref[0] + _dot(cat, wout_ref[...])


def _mixer_b_out(x, tok, q_mem, kvbd, layer, w_out, tm):
    batch, seq, _ = x.shape

    def tile(width):
        return pl.BlockSpec((1, tm, width), lambda b, s: (b, s, 0))

    kvspec = pl.BlockSpec((1, 1, N_MEM_HEADS * N_MEM, MEM_WIDTH), lambda b, s: (layer, b, 0, 0))
    return pl.pallas_call(
        _mixer_b_out_kernel,
        grid=(batch, seq // tm),
        in_specs=[tile(D_MODEL), tile(TOK_WIDTH), tile(MEM_WIDTH), kvspec, kvspec,
                  _const_spec(w_out.shape)],
        out_specs=tile(D_MODEL),
        out_shape=jax.ShapeDtypeStruct(x.shape, F32),
        compiler_params=_params(2),
        name="mixer_b_out",
    )(x, tok, q_mem, kvbd[0], kvbd[1], w_out)


def kernel(x, mem, a_norm1, a_w_in, a_v_norm, a_w_s, a_b_s, a_mem_norm, a_w_mem_kv, a_w_out, a_norm2, a_ffn_in, a_ffn_conv, a_ffn_conv_b, a_ffn_out, kv_norm, w_kv, b_f, b_norm1, b_w_q, b_mem_norm, b_w_mem_kv, b_w_out, b_norm2, b_ffn_in, b_ffn_conv, b_ffn_conv_b, b_ffn_out, final_norm):
    n_a, n_b = a_norm1.shape[0], b_norm1.shape[0]
    assert n_a == 1 and n_b == 1, "one gMLP layer, then one forgetting-attention layer"
    batch, seq, _ = x.shape
    row = lambda a: a.reshape(1, -1)
    bf = lambda a: a.astype(BF16)

    kvbd = _memkv(mem,
                  jnp.concatenate([a_mem_norm, b_mem_norm], axis=0)[:, None, :],
                  bf(jnp.concatenate([a_w_mem_kv, b_w_mem_kv], axis=0)))

    b_full = jnp.repeat(a_b_s[0].T, TOK_WIDTH // GMLP_GROUPS, axis=1)
    x, (ffn_in, ffn_out) = _mixer_a(
        x, row(a_norm1[0]), bf(a_w_in[0]), row(a_v_norm[0]), a_w_s[0], b_full,
        kvbd, 0, bf(a_w_out[0]), (a_ffn_in[0], a_ffn_out[0]), MIX_A_TM)
    x = _ffn(x, row(a_norm2[0]), ffn_in, a_ffn_conv[0], row(a_ffn_conv_b[0]),
             ffn_out, row(final_norm), FFN_TM, False)

    w_f = jnp.pad(w_kv[:, 2 * TOK_WIDTH:], ((0, 0), (0, LANES - N_FOX_HEADS)))
    b_f_pad = jnp.pad(b_f, (0, LANES - N_FOX_HEADS)).reshape(1, LANES)

    for j in range(n_b):
        k_sh, v_sh, c_tok, c_head, q_fox, q_mem = _proj_b(
            x, row(kv_norm), bf(w_kv[:, :2 * TOK_WIDTH]), bf(w_f), b_f_pad,
            row(b_norm1[j]), bf(b_w_q[j]), PROJ_TM)
        tok, (ffn_in, ffn_out) = _fox_attention(
            q_fox, k_sh, v_sh, c_tok, c_head, (b_ffn_in[j], b_ffn_out[j]), FOX_TQ)
        x = _mixer_b_out(x, tok, q_mem, kvbd, n_a + j, bf(b_w_out[j]), MIX_B_TM)
        x = _ffn(x, row(b_norm2[j]), ffn_in, b_ffn_conv[j], row(b_ffn_conv_b[j]),
                 ffn_out, row(final_norm), FFN_TM, j == n_b - 1)
    return x
```

```python
import functools

import jax
import jax.numpy as jnp
from jax import lax
from jax.experimental import pallas as pl
from jax.experimental.pallas import tpu as pltpu

D_MODEL = 1024
N_MEM = 256
HEAD_DIM = 64
N_MEM_HEADS = 4
MEM_WIDTH = N_MEM_HEADS * HEAD_DIM
TOK_WIDTH = D_MODEL - MEM_WIDTH
N_FOX_HEADS = TOK_WIDTH // HEAD_DIM
GMLP_BLOCK = 128
GMLP_GROUPS = 4
D_FF = 2816
CONV_WIDTH = 3
EPS = 1e-6
Q_SCALE = HEAD_DIM ** -0.5

LANES = 128
SUBLANES = 8
HEAD_PAIR = 2 * HEAD_DIM
FF_CHUNK = 256
FOX_TQ = 512
FOX_PAIRS = 3
FOX_TK = 512
FFN_TM = 512
CUMSUM_ROWS = 256
PROJ_TM = 1024
MIX_A_TM = 1024
IN_CHUNK = 256
MIX_B_TM = 1024
CAST_BLOCKS = 16
MEMKV_B = 4
MEM_ROWS = 256
VMEM_LIMIT = 56 * 1024 * 1024

BF16 = jnp.bfloat16
F32 = jnp.float32


def _const_spec(shape):
    nd = len(shape)
    return pl.BlockSpec(shape, lambda *_: (0,) * nd, pipeline_mode=pl.Buffered(1))


def _params(n_axes):
    return pltpu.CompilerParams(
        dimension_semantics=("arbitrary",) * n_axes, vmem_limit_bytes=VMEM_LIMIT)


def _rms(x, g):
    r = lax.rsqrt(jnp.mean(x * x, axis=-1, keepdims=True) + EPS)
    return x * r * g


def _dot(a, b):
    return jnp.dot(a, b, preferred_element_type=F32)


def _dot_nt(a, b):
    return lax.dot_general(a, b, (((1,), (1,)), ((), ())), preferred_element_type=F32)


def _memkv_kernel(mem_ref, g_ref, w_ref, kbd_ref, vbd_ref):
    n_layers = w_ref.shape[0]
    feat_head = lax.broadcasted_iota(jnp.int32, (N_MEM, MEM_WIDTH), 1) // HEAD_DIM
    for b in range(mem_ref.shape[0]):
        m = mem_ref[b]
        mr = m * lax.rsqrt(jnp.mean(m * m, axis=-1, keepdims=True) + EPS)
        for l in range(n_layers):
            kv = _dot((mr * g_ref[l]).astype(BF16), w_ref[l])
            k, v = kv[:, :MEM_WIDTH], kv[:, MEM_WIDTH:]
            for h in range(N_MEM_HEADS):
                rows = slice(h * N_MEM, (h + 1) * N_MEM)
                kbd_ref[l, b, rows, :] = jnp.where(feat_head == h, k, 0.0).astype(BF16)
                vbd_ref[l, b, rows, :] = jnp.where(feat_head == h, v, 0.0).astype(BF16)


def _memkv(mem, norms, ws):
    n_layers, batch = norms.shape[0], mem.shape[0]
    out = jax.ShapeDtypeStruct((n_layers, batch, N_MEM_HEADS * N_MEM, MEM_WIDTH), BF16)
    out_spec = pl.BlockSpec((n_layers, MEMKV_B, N_MEM_HEADS * N_MEM, MEM_WIDTH),
                            lambda b: (0, b, 0, 0))
    return pl.pallas_call(
        _memkv_kernel,
        grid=(batch // MEMKV_B,),
        in_specs=[
            pl.BlockSpec((MEMKV_B, N_MEM, D_MODEL), lambda b: (b, 0, 0)),
            _const_spec(norms.shape),
            _const_spec(ws.shape),
        ],
        out_specs=[out_spec, out_spec],
        out_shape=[out, out],
        compiler_params=_params(1),
        name="memkv",
    )(mem, norms, ws)


def _mem_softmax(logits):
    probs = []
    for h in range(N_MEM_HEADS):
        seg = logits[:, h * N_MEM:(h + 1) * N_MEM]
        e = jnp.exp(seg - jnp.max(seg, axis=-1, keepdims=True))
        probs.append((e / jnp.sum(e, axis=-1, keepdims=True)).astype(BF16))
    return jnp.concatenate(probs, axis=-1)


def _mem_attention(qs, kbd_ref, vbd_ref):
    logits = [_dot_nt(q, kbd_ref[0, 0]) for q in qs]
    probs = [_mem_softmax(l) for l in logits]
    return [_dot(p, vbd_ref[0, 0]) for p in probs]


def _gmlp_mix(vn, ws_ref):
    tm = vn.shape[0]
    tri = (lax.broadcasted_iota(jnp.int32, (GMLP_BLOCK, GMLP_BLOCK), 0)
           >= lax.broadcasted_iota(jnp.int32, (GMLP_BLOCK, GMLP_BLOCK), 1))
    w = [jnp.where(tri, ws_ref[g], 0.0).astype(BF16) for g in range(GMLP_GROUPS)]
    first_half = lax.broadcasted_iota(jnp.int32, (GMLP_BLOCK, LANES), 1) < HEAD_DIM
    win_lo = (0, 128, 384, 512)
    blocks = []
    for r in range(tm // GMLP_BLOCK):
        rows = slice(r * GMLP_BLOCK, (r + 1) * GMLP_BLOCK)
        m = [_dot(w[g], vn[rows, win_lo[g]:win_lo[g] + 2 * LANES]) for g in range(GMLP_GROUPS)]
        blocks.append(jnp.concatenate([
            m[0][:, :LANES],
            jnp.where(first_half, m[0][:, LANES:], m[1][:, :LANES]),
            m[1][:, LANES:],
            m[2][:, :LANES],
            jnp.where(first_half, m[2][:, LANES:], m[3][:, :LANES]),
            m[3][:, LANES:],
        ], axis=-1))
    return jnp.concatenate(blocks, axis=0)


def _cast_spec(w, step_of, n_steps):
    rows = w.shape[0] // CAST_BLOCKS
    return pl.BlockSpec((rows, w.shape[1]),
                        lambda *idx: (step_of(*idx) * CAST_BLOCKS // n_steps, 0))


def _mixer_a_kernel(x_ref, g1_ref, win_ref, vg_ref, ws_ref, bs_ref, kbd_ref, vbd_ref, wout_ref,
                    fi_ref, fo_ref, o_ref, fi_bf_ref, fo_bf_ref):
    fi_bf_ref[...] = fi_ref[...].astype(BF16)
    fo_bf_ref[...] = fo_ref[...].astype(BF16)
    x = x_ref[0]
    tm = x.shape[0]
    h = _rms(x, g1_ref[...]).astype(BF16)
    n_tok = TOK_WIDTH // IN_CHUNK

    def z_chunk(c):
        return _dot(h, win_ref[:, c * IN_CHUNK:(c + 1) * IN_CHUNK])

    zv = [z_chunk(n_tok)]
    v = []
    for c in range(1, n_tok):
        zv.append(z_chunk(n_tok + c))
        v.append(jax.nn.gelu(zv[c - 1]))
    zq = z_chunk(2 * n_tok)
    v.append(jax.nn.gelu(zv[n_tok - 1]))
    zu = [z_chunk(0)]
    ssq = sum(jnp.sum(vc * vc, axis=-1, keepdims=True) for vc in v)
    r = lax.rsqrt(ssq * (1.0 / TOK_WIDTH) + EPS)
    vn = jnp.concatenate(
        [(vc * r * vg_ref[:, c * IN_CHUNK:(c + 1) * IN_CHUNK]).astype(BF16)
         for c, vc in enumerate(v)], axis=-1)
    q_mem = (zq * Q_SCALE).astype(BF16)
    blocks = [slice(r0, r0 + MEM_ROWS) for r0 in range(0, tm, MEM_ROWS)]
    logits = [_dot_nt(q_mem[rows], kbd_ref[0, 0]) for rows in blocks]
    u, probs = [], []
    for c in range(1, n_tok):
        zu.append(z_chunk(c))
        u.append(jax.nn.gelu(zu[c - 1]))
        if c - 1 < len(logits):
            probs.append(_mem_softmax(logits[c - 1]))
    probs.extend(_mem_softmax(l) for l in logits[len(probs):])
    mem_o = [_dot(p, vbd_ref[0, 0]) for p in probs]
    u.append(jax.nn.gelu(zu[n_tok - 1]))
    bias = jnp.concatenate([bs_ref[...]] * (tm // GMLP_BLOCK), axis=0)
    tok = (jnp.concatenate(u, axis=-1) * (_gmlp_mix(vn, ws_ref) + bias)).astype(BF16)
    cat = jnp.concatenate([tok, jnp.concatenate(mem_o, axis=0).astype(BF16)], axis=-1)
    o_ref[0] = x + _dot(cat, wout_ref[...])


def _mixer_a(x, g1, w_in, v_g, w_s, b_full, kvbd, layer, w_out, ffn_weights, tm):
    batch, seq, _ = x.shape
    n_s = seq // tm
    xspec = pl.BlockSpec((1, tm, D_MODEL), lambda b, s: (b, s, 0))
    kvspec = pl.BlockSpec((1, 1, N_MEM_HEADS * N_MEM, MEM_WIDTH), lambda b, s: (layer, b, 0, 0))
    cast_specs = [_cast_spec(w, lambda b, s: b * n_s + s, batch * n_s) for w in ffn_weights]
    cast_shapes = [jax.ShapeDtypeStruct(w.shape, BF16) for w in ffn_weights]
    out, *casts = pl.pallas_call(
        _mixer_a_kernel,
        grid=(batch, n_s),
        in_specs=[
            xspec,
            _const_spec((1, D_MODEL)),
            _const_spec(w_in.shape),
            _const_spec((1, TOK_WIDTH)),
            _const_spec(w_s.shape),
            _const_spec(b_full.shape),
            kvspec, kvspec,
            _const_spec(w_out.shape),
        ] + cast_specs,
        out_specs=[xspec] + cast_specs,
        out_shape=[jax.ShapeDtypeStruct(x.shape, F32)] + cast_shapes,
        compiler_params=_params(2),
        name="mixer_a",
    )(x, g1, w_in, v_g, w_s, b_full, kvbd[0], kvbd[1], w_out, *ffn_weights)
    return out, casts


def _shift_rows(h, prev, n):
    rolled = pltpu.roll(h, n, 0)
    head = jnp.where(lax.broadcasted_iota(jnp.int32, prev.shape, 0) < n,
                     pltpu.roll(prev, n, 0), rolled[:SUBLANES])
    return jnp.concatenate([head, rolled[SUBLANES:]], axis=0)


def _ffn_kernel(*refs, final_norm, n_carried):
    x_ref, g_ref, win_ref, cw_ref, cb_ref, wout_ref, fn_ref = refs[:7]
    cast_in, o_ref = refs[7:7 + n_carried], refs[7 + n_carried]
    cast_out = refs[8 + n_carried:8 + 2 * n_carried]
    carry_ref, act_ref = refs[8 + 2 * n_carried:]
    for src, dst in zip(cast_in, cast_out):
        dst[...] = src[...].astype(BF16)

    @pl.when(pl.program_id(1) == 0)
    def _():
        carry_ref[...] = jnp.zeros_like(carry_ref)

    x = x_ref[0]
    tm = x.shape[0]
    hn = _rms(x, g_ref[...]).astype(BF16)

    def up_proj(col0):
        return _dot(hn, win_ref[:, col0:col0 + FF_CHUNK])

    def conv_cols(h, col0):
        cols = slice(col0, col0 + FF_CHUNK)
        prev = carry_ref[:, cols]
        carry_ref[:, cols] = h[tm - SUBLANES:]
        w = cw_ref[:, cols]
        return (cb_ref[:, cols] + w[2:3] * h
                + w[0:1] * _shift_rows(h, prev, 2) + w[1:2] * _shift_rows(h, prev, 1))

    for c in range(D_FF // FF_CHUNK):
        gate = conv_cols(up_proj(c * FF_CHUNK), c * FF_CHUNK)
        up = conv_cols(up_proj(D_FF + c * FF_CHUNK), D_FF + c * FF_CHUNK)
        half_gate = 0.5 * gate
        act_ref[:, c * FF_CHUNK:(c + 1) * FF_CHUNK] = (
            (half_gate + half_gate * jnp.tanh(half_gate)) * up).astype(BF16)
    y = x + _dot(act_ref[...], wout_ref[...])
    if final_norm:
        y = _rms(y, fn_ref[...])
    o_ref[0] = y


def _ffn(x, g, w_in, conv_w, conv_b, w_out, fn, tm, final_norm, carried=()):
    batch, seq, _ = x.shape
    n_s = seq // tm
    xspec = pl.BlockSpec((1, tm, D_MODEL), lambda b, s: (b, s, 0))
    cast_specs = [_cast_spec(w, lambda b, s: b * n_s + s, batch * n_s) for w in carried]
    cast_shapes = [jax.ShapeDtypeStruct(w.shape, BF16) for w in carried]
    out, *casts = pl.pallas_call(
        functools.partial(_ffn_kernel, final_norm=final_norm, n_carried=len(carried)),
        grid=(batch, n_s),
        in_specs=[
            xspec,
            _const_spec((1, D_MODEL)),
            _const_spec(w_in.shape),
            _const_spec(conv_w.shape),
            _const_spec(conv_b.shape),
            _const_spec(w_out.shape),
            _const_spec((1, D_MODEL)),
        ] + cast_specs,
        out_specs=[xspec] + cast_specs,
        out_shape=[jax.ShapeDtypeStruct(x.shape, F32)] + cast_shapes,
        scratch_shapes=[pltpu.VMEM((SUBLANES, 2 * D_FF), F32), pltpu.VMEM((tm, D_FF), BF16)],
        compiler_params=_params(2),
        name="ffn_final" if final_norm else "ffn",
    )(x, g, w_in, conv_w, conv_b, w_out, fn, *carried)
    return out, casts


def _split3(x):
    hi = x.astype(BF16)
    r = x - hi.astype(F32)
    mid = r.astype(BF16)
    lo = (r - mid.astype(F32)).astype(BF16)
    return hi, mid, lo


def _proj_b_kernel(x_ref, gkv_ref, wkv_ref, wf_ref, bf_ref, gq_ref, wq_ref,
                   k_ref, v_ref, ct_ref, cr_ref, qf_ref, qm_ref, carry_ref):
    @pl.when(pl.program_id(1) == 0)
    def _():
        carry_ref[...] = jnp.zeros_like(carry_ref)

    x = x_ref[0]
    tm = x.shape[0]
    xr = x * lax.rsqrt(jnp.mean(x * x, axis=-1, keepdims=True) + EPS)
    hkv = (xr * gkv_ref[...]).astype(BF16)
    hq = (xr * gq_ref[...]).astype(BF16)

    def project(h, w_ref, out_ref, c, col0, scale=None):
        z = _dot(h, w_ref[:, col0 + c * IN_CHUNK:col0 + (c + 1) * IN_CHUNK])
        if scale is not None:
            z = z * scale
        out_ref[0, :, c * IN_CHUNK:(c + 1) * IN_CHUNK] = z.astype(BF16)

    n_tok = TOK_WIDTH // IN_CHUNK
    f = _dot(hkv, wf_ref[...]) + bf_ref[...]
    project(hkv, wkv_ref, k_ref, 0, 0)
    log_f = jnp.minimum(f, 0.0) - jnp.log1p(jnp.exp(-jnp.abs(f)))
    tri = (lax.broadcasted_iota(jnp.int32, (CUMSUM_ROWS, CUMSUM_ROWS), 0)
           >= lax.broadcasted_iota(jnp.int32, (CUMSUM_ROWS, CUMSUM_ROWS), 1)).astype(BF16)
    parts = _split3(log_f)
    project(hkv, wkv_ref, k_ref, 1, 0)
    carry, c_blocks = carry_ref[0:1, :], []
    for r0 in range(0, tm, CUMSUM_ROWS):
        hi, mid, lo = (part[r0:r0 + CUMSUM_ROWS] for part in parts)
        c_blk = carry + ((_dot(tri, hi) + _dot(tri, mid)) + _dot(tri, lo))
        carry = c_blk[CUMSUM_ROWS - 1:CUMSUM_ROWS, :]
        c_blocks.append(c_blk)
    c = jnp.concatenate(c_blocks, axis=0)
    project(hkv, wkv_ref, k_ref, 2, 0)
    carry_ref[...] = jnp.broadcast_to(carry, carry_ref.shape)
    ct_ref[0] = c
    cr_ref[0] = c.T[:2 * SUBLANES, :]
    for j in range(n_tok):
        project(hkv, wkv_ref, v_ref, j, TOK_WIDTH)
    for j in range(n_tok):
        project(hq, wq_ref, qf_ref, j, 0, Q_SCALE)
    project(hq, wq_ref, qm_ref, 0, TOK_WIDTH, Q_SCALE)


def _proj_b(x, g_kv, w_kv, w_f, b_f, g_q, w_q, tm):
    batch, seq, _ = x.shape

    def tile(width):
        return pl.BlockSpec((1, tm, width), lambda b, s: (b, s, 0))

    return pl.pallas_call(
        _proj_b_kernel,
        grid=(batch, seq // tm),
        in_specs=[
            tile(D_MODEL),
            _const_spec((1, D_MODEL)),
            _const_spec(w_kv.shape),
            _const_spec(w_f.shape),
            _const_spec((1, LANES)),
            _const_spec((1, D_MODEL)),
            _const_spec(w_q.shape),
        ],
        out_specs=[
            tile(TOK_WIDTH), tile(TOK_WIDTH), tile(LANES),
            pl.BlockSpec((1, 2 * SUBLANES, tm), lambda b, s: (b, 0, s)),
            tile(TOK_WIDTH), tile(MEM_WIDTH),
        ],
        out_shape=[
            jax.ShapeDtypeStruct((batch, seq, TOK_WIDTH), BF16),
            jax.ShapeDtypeStruct((batch, seq, TOK_WIDTH), BF16),
            jax.ShapeDtypeStruct((batch, seq, LANES), F32),
            jax.ShapeDtypeStruct((batch, 2 * SUBLANES, seq), F32),
            jax.ShapeDtypeStruct((batch, seq, TOK_WIDTH), BF16),
            jax.ShapeDtypeStruct((batch, seq, MEM_WIDTH), BF16),
        ],
        scratch_shapes=[pltpu.VMEM((SUBLANES, LANES), F32)],
        compiler_params=_params(2),
        name="proj_b",
    )(x, g_kv, w_kv, w_f, b_f, g_q, w_q)


def _fox_tile(n_keys, group, q_ref, k_ref, v_ref, ct_ref, cr_ref, o_ref):
    tq, tk = q_ref.shape[1], FOX_TK
    n_tiles = n_keys // tk
    ct_all = ct_ref[0]
    lane = lax.broadcasted_iota(jnp.int32, (tq, LANES), 1)
    low_half = lane < HEAD_DIM
    v_low = lax.broadcasted_iota(jnp.int32, (tk, LANES), 1) < HEAD_DIM
    q_pos = lax.broadcasted_iota(jnp.int32, (tq, tk), 0) + (n_keys - tq)
    k_pos = lax.broadcasted_iota(jnp.int32, (tq, tk), 1)
    heads = range(2 * FOX_PAIRS)

    def pair_lanes(hh):
        return slice((hh // 2) * HEAD_PAIR, (hh // 2 + 1) * HEAD_PAIR)

    qh, ct = [], []
    for hh in heads:
        q2 = q_ref[0, :, pair_lanes(hh)]
        qh.append(jnp.where(low_half if hh % 2 == 0 else ~low_half, q2, jnp.zeros_like(q2)))
        head = 2 * FOX_PAIRS * group + hh
        ct.append(jnp.sum(jnp.where(lane == head, ct_all, 0.0), axis=-1, keepdims=True))

    s = [[None] * n_tiles for _ in heads]
    m_rows = [None for _ in heads]
    shift = [None for _ in heads]
    pv = [None for _ in heads]

    def pass1(hh, t):
        keys = slice(t * tk, (t + 1) * tk)
        k_t = k_ref[0, keys, pair_lanes(hh)]
        s_t = _dot_nt(qh[hh], k_t) - cr_ref[0, pl.ds(2 * FOX_PAIRS * group + hh, 1), keys]
        if (t + 1) * tk > n_keys - tq:
            s_t = jnp.where(q_pos >= k_pos + t * tk, s_t, -jnp.inf)
        s[hh][t] = s_t
        m_t = jnp.max(s_t, axis=-1, keepdims=True)
        m_rows[hh] = m_t if t == 0 else jnp.maximum(m_rows[hh], m_t)

    def pass2(hh, t):
        if t == 0:
            shift[hh] = ct[hh] - (m_rows[hh] + ct[hh])
        v_t = v_ref[0, t * tk:(t + 1) * tk, pair_lanes(hh)]
        p_t = jnp.exp(s[hh][t] + shift[hh]).astype(BF16)
        v_h = jnp.where(v_low if hh % 2 == 0 else ~v_low, v_t, jnp.ones_like(v_t))
        pv_t = _dot(p_t, v_h)
        pv[hh] = pv_t if t == 0 else pv[hh] + pv_t

    for pp in range(FOX_PAIRS + 1):
        for t in range(n_tiles):
            for hh in (2 * pp, 2 * pp + 1):
                if pp < FOX_PAIRS:
                    pass1(hh, t)
            for hh in (2 * pp - 2, 2 * pp - 1):
                if pp >= 1:
                    pass2(hh, t)
    outs = [pv[hh] / pltpu.roll(pv[hh], HEAD_DIM, 1) for hh in heads]
    for pp in range(FOX_PAIRS):
        o_ref[0, :, pp * HEAD_PAIR:(pp + 1) * HEAD_PAIR] = jnp.where(
            low_half, outs[2 * pp], outs[2 * pp + 1]).astype(BF16)


def _fox_kernel(q_ref, k_ref, v_ref, ct_ref, cr_ref, fi_ref, fo_ref, o_ref, fi_bf_ref, fo_bf_ref):
    fi_bf_ref[...] = fi_ref[...].astype(BF16)
    fo_bf_ref[...] = fo_ref[...].astype(BF16)
    tq, seq = q_ref.shape[1], k_ref.shape[1]
    group = pl.program_id(1)
    qi = pl.program_id(2)
    for i in range(seq // tq):
        pl.when(qi == i)(functools.partial(
            _fox_tile, (i + 1) * tq, group, q_ref, k_ref, v_ref, ct_ref, cr_ref, o_ref))


def _fox_attention(q, k, v, c_tok, c_rows, ffn_weights, tq):
    batch, seq, _ = q.shape
    width = FOX_PAIRS * HEAD_PAIR
    n_p, n_i = TOK_WIDTH // width, seq // tq
    qspec = pl.BlockSpec((1, tq, width), lambda b, p, i: (b, i, p))
    kvspec = pl.BlockSpec((1, seq, width), lambda b, p, i: (b, 0, p))
    cast_specs = [_cast_spec(w, lambda b, p, i: (b * n_p + p) * n_i + i, batch * n_p * n_i)
                  for w in ffn_weights]
    cast_shapes = [jax.ShapeDtypeStruct(w.shape, BF16) for w in ffn_weights]
    out, *casts = pl.pallas_call(
        _fox_kernel,
        grid=(batch, n_p, n_i),
        in_specs=[
            qspec, kvspec, kvspec,
            pl.BlockSpec((1, tq, LANES), lambda b, p, i: (b, i, 0)),
            pl.BlockSpec((1, 2 * SUBLANES, seq), lambda b, p, i: (b, 0, 0)),
        ] + cast_specs,
        out_specs=[qspec] + cast_specs,
        out_shape=[jax.ShapeDtypeStruct((batch, seq, TOK_WIDTH), BF16)] + cast_shapes,
        compiler_params=_params(3),
        name="fox_attention",
    )(q, k, v, c_tok, c_rows, *ffn_weights)
    return out, casts


def _mixer_b_out_kernel(x_ref, tok_ref, qm_ref, kbd_ref, vbd_ref, wout_ref, o_ref):
    tm = x_ref.shape[1]
    blocks = [slice(r, r + MEM_ROWS) for r in range(0, tm, MEM_ROWS)]
    mem_o = _mem_attention([qm_ref[0, rows, :] for rows in blocks], kbd_ref, vbd_ref)
    mem_o = jnp.concatenate(mem_o, axis=0).astype(BF16)
    cat = jnp.concatenate([tok_ref[0], mem_o], axis=-1)
    o_ref[0] = x_ref[0] + _dot(cat, wout_ref[...])


def _mixer_b_out(x, tok, q_mem, kvbd, layer, w_out, tm):
    batch, seq, _ = x.shape

    def tile(width):
        return pl.BlockSpec((1, tm, width), lambda b, s: (b, s, 0))

    kvspec = pl.BlockSpec((1, 1, N_MEM_HEADS * N_MEM, MEM_WIDTH), lambda b, s: (layer, b, 0, 0))
    return pl.pallas_call(
        _mixer_b_out_kernel,
        grid=(batch, seq // tm),
        in_specs=[tile(D_MODEL), tile(TOK_WIDTH), tile(MEM_WIDTH), kvspec, kvspec,
                  _const_spec(w_out.shape)],
        out_specs=tile(D_MODEL),
        out_shape=jax.ShapeDtypeStruct(x.shape, F32),
        compiler_params=_params(2),
        name="mixer_b_out",
    )(x, tok, q_mem, kvbd[0], kvbd[1], w_out)


def kernel(x, mem, a_norm1, a_w_in, a_v_norm, a_w_s, a_b_s, a_mem_norm, a_w_mem_kv, a_w_out, a_norm2, a_ffn_in, a_ffn_conv, a_ffn_conv_b, a_ffn_out, kv_norm, w_kv, b_f, b_norm1, b_w_q, b_mem_norm, b_w_mem_kv, b_w_out, b_norm2, b_ffn_in, b_ffn_conv, b_ffn_conv_b, b_ffn_out, final_norm):
    n_a, n_b = a_norm1.shape[0], b_norm1.shape[0]
    assert n_a == 1 and n_b == 1, "one gMLP layer, then one forgetting-attention layer"
    batch, seq, _ = x.shape
    row = lambda a: a.reshape(1, -1)
    bf = lambda a: a.astype(BF16)

    kvbd = _memkv(mem,
                  jnp.concatenate([a_mem_norm, b_mem_norm], axis=0)[:, None, :],
                  bf(jnp.concatenate([a_w_mem_kv, b_w_mem_kv], axis=0)))

    b_full = jnp.repeat(a_b_s[0].T, TOK_WIDTH // GMLP_GROUPS, axis=1)
    x, (ffn_in, ffn_out) = _mixer_a(
        x, row(a_norm1[0]), bf(a_w_in[0]), row(a_v_norm[0]), a_w_s[0], b_full,
        kvbd, 0, bf(a_w_out[0]), (a_ffn_in[0], a_ffn_out[0]), MIX_A_TM)
    x, (w_kv_bf, w_q_bf, w_out_b_bf) = _ffn(
        x, row(a_norm2[0]), ffn_in, a_ffn_conv[0], row(a_ffn_conv_b[0]),
        ffn_out, row(final_norm), FFN_TM, False, carried=(w_kv, b_w_q[0], b_w_out[0]))

    w_f = jnp.pad(w_kv[:, 2 * TOK_WIDTH:], ((0, 0), (0, LANES - N_FOX_HEADS)))
    b_f_pad = jnp.pad(b_f, (0, LANES - N_FOX_HEADS)).reshape(1, LANES)

    k_sh, v_sh, c_tok, c_head, q_fox, q_mem = _proj_b(
        x, row(kv_norm), w_kv_bf, bf(w_f), b_f_pad, row(b_norm1[0]), w_q_bf, PROJ_TM)
    tok, (ffn_in, ffn_out) = _fox_attention(
        q_fox, k_sh, v_sh, c_tok, c_head, (b_ffn_in[0], b_ffn_out[0]), FOX_TQ)
    x = _mixer_b_out(x, tok, q_mem, kvbd, n_a, w_out_b_bf, MIX_B_TM)
    x, _ = _ffn(x, row(b_norm2[0]), ffn_in, b_ffn_conv[0], row(b_ffn_conv_b[0]),
                ffn_out, row(final_norm), FFN_TM, True)
    return x
```
